```python
import jax
import jax.numpy as jnp
from jax import lax
import numpy as np

D_MODEL = 1024
BATCH = 8
SEQ = 2048
DEPTH = 4
DEC_BATCH = 128
DEC_SEQ = 4
PAST_LEN = 2048
PAGE_SIZE = 128

N_A = DEPTH // 2
N_B = DEPTH - N_A
HEAD_DIM = 64
N_HEADS = D_MODEL // HEAD_DIM
A_DECAY_LORA = 64
A_AAA_LORA = 64
A_MV_LORA = 32
A_GATE_LORA = 128
N_EXPERTS = 16
N_GROUPS = 4
EXPERTS_PER_GROUP = N_EXPERTS // N_GROUPS
TOP_K = 2
D_EXPERT = 512
Q_BLOCK = 128
RMS_EPS = 1e-6
GN_EPS = 64e-5

kernel_name = 'yoco_rwkv7_fox_moe_step'


def rms_norm(x, g):
    xf = x.astype(jnp.float32)
    y = xf * lax.rsqrt(jnp.mean(xf * xf, axis=-1, keepdims=True) + RMS_EPS)
    return (y * g.astype(jnp.float32)).astype(x.dtype)


def adaln_params(c, w, b, n):
    m = jnp.dot(jax.nn.silu(c), w) + b
    return jnp.split(m, n, axis=-1)


def modulate(x, g, shift, scale):
    return rms_norm(x, g) * (1 + scale[:, None, :]) + shift[:, None, :]


def wkv_step(S, inp):
    r_t, w_t, k_t, v_t, a_t, b_t = inp
    sa = jnp.einsum('bhvk,bhk->bhv', S, a_t)
    S = S * w_t[:, :, None, :] + sa[..., None] * b_t[:, :, None, :] + v_t[..., None] * k_t[:, :, None, :]
    return S, jnp.einsum('bhvk,bhk->bhv', S, r_t)


def rwkv7_time_mix(xn, shift_prev, wkv0, v_first, v_mix, mu, w0, w1, w2, a0, a1, a2,
                   g1, g2, k_k, k_a, r_k, wr, wk, wv, wo, lnx_w, lnx_b):
    f32 = jnp.float32
    bsz, t_len, d = xn.shape
    x_prev = jnp.concatenate([shift_prev[:, None, :].astype(xn.dtype), xn[:, :-1]], axis=1)
    xx = x_prev - xn
    xr = xn + xx * mu[0]
    xw = xn + xx * mu[1]
    xk = xn + xx * mu[2]
    xv = xn + xx * mu[3]
    xa = xn + xx * mu[4]
    xg = xn + xx * mu[5]
    r = xr @ wr
    w_log = -jax.nn.softplus(-(w0 + jnp.tanh(xw @ w1) @ w2)) - 0.5
    k = xk @ wk
    v = xv @ wv
    if v_mix is None:
        v_first = v
    else:
        v0, v1, v2 = v_mix
        v = v + (v_first - v) * jax.nn.sigmoid(v0 + (xv @ v1) @ v2)
    a = jax.nn.sigmoid(a0 + (xa @ a1) @ a2)
    g = jax.nn.sigmoid(xg @ g1) @ g2

    def heads(z):
        return z.reshape(bsz, t_len, N_HEADS, HEAD_DIM).astype(f32)

    kk = heads(k * k_k)
    kk = kk * lax.rsqrt(jnp.maximum(jnp.sum(kk * kk, axis=-1, keepdims=True), 1e-24))
    k = k * (1 + (a - 1) * k_a)
    r_h, k_h, v_h, a_h = heads(r), heads(k), heads(v), heads(a)
    decay = jnp.exp(-jnp.exp(heads(w_log)))

    def seq(z):
        return jnp.moveaxis(z, 1, 0)

    S, y = lax.scan(wkv_step, wkv0.astype(f32),
                    (seq(r_h), seq(decay), seq(k_h), seq(v_h), seq(-kk), seq(kk * a_h)))
    y = jnp.moveaxis(y, 0, 1)
    mean = jnp.mean(y, axis=-1, keepdims=True)
    var = jnp.mean(jnp.square(y - mean), axis=-1, keepdims=True)
    y = ((y - mean) * lax.rsqrt(var + GN_EPS)).reshape(bsz, t_len, d) * lnx_w.astype(f32) + lnx_b.astype(f32)
    bonus = jnp.sum(r_h * k_h * r_k.astype(f32), axis=-1, keepdims=True) * v_h
    y = (y + bonus.reshape(bsz, t_len, d)).astype(xn.dtype)
    out = (y * g) @ wo
    return out, v_first, xn[:, -1], S


def moe_ffn(xn, router_w, router_bias, w_gate, w_up, w_down):
    f32 = jnp.float32
    bsz, t_len, d = xn.shape
    xt = xn.reshape(-1, d)
    probs = jax.nn.softmax(jnp.dot(xt.astype(f32), router_w.astype(f32)), axis=-1)
    sel = probs + router_bias.astype(f32)
    grp_score = jnp.sum(lax.top_k(sel.reshape(-1, N_GROUPS, EXPERTS_PER_GROUP), TOP_K)[0], axis=-1)
    best = jnp.argmax(grp_score, axis=-1)
    in_group = (jnp.arange(N_EXPERTS) // EXPERTS_PER_GROUP)[None, :] == best[:, None]
    _, idx = lax.top_k(jnp.where(in_group, sel, -jnp.inf), TOP_K)
    w_sel = jnp.take_along_axis(probs, idx, axis=-1)
    w_sel = w_sel / jnp.sum(w_sel, axis=-1, keepdims=True)
    gate = jnp.sum(jax.nn.one_hot(idx, N_EXPERTS, dtype=f32) * w_sel[..., None], axis=1)
    h = jax.nn.silu(jnp.einsum('nd,edf->nef', xt, w_gate)) * jnp.einsum('nd,edf->nef', xt, w_up)
    h = h * gate[..., None].astype(h.dtype)
    y = jnp.einsum('nef,efd->nd', h, w_down)
    return y.reshape(bsz, t_len, d)


def shared_kv(x, c, kv_mod_w, kv_mod_b, kv_norm_g, kv_w, kv_b_f, k_norm_g):
    bsz, t_len, d = x.shape
    shift, scale = adaln_params(c, kv_mod_w, kv_mod_b, 2)
    proj = modulate(x, kv_norm_g, shift, scale) @ kv_w
    k = rms_norm(proj[..., :d].reshape(bsz, t_len, N_HEADS, HEAD_DIM), k_norm_g)
    v = proj[..., d:2 * d].reshape(bsz, t_len, N_HEADS, HEAD_DIM)
    logf = jax.nn.log_sigmoid((proj[..., 2 * d:] + kv_b_f).astype(jnp.float32))
    return k, v, logf


def fox_prompt_attention(q, k, v, logf):
    f32 = jnp.float32
    bsz, s_len, n_h, hd = q.shape
    nb = s_len // Q_BLOCK
    F = jnp.cumsum(logf.astype(f32), axis=1)
    F_k = jnp.transpose(F, (0, 2, 1))
    kf = k.astype(f32)
    vf = v.astype(f32)
    pos_k = jnp.arange(s_len)
    q_blocks = q.astype(f32).reshape(bsz, nb, Q_BLOCK, n_h, hd).transpose(1, 0, 2, 3, 4)
    F_blocks = F.reshape(bsz, nb, Q_BLOCK, n_h).transpose(1, 0, 3, 2)
    pos_blocks = pos_k.reshape(nb, Q_BLOCK)
    scale = hd ** -0.5

    def block(args):
        qb, Fb, pb = args
        s = jnp.einsum('bqhd,bkhd->bhqk', qb, kf) * scale + Fb[..., None] - F_k[:, :, None, :]
        s = jnp.where(pos_k[None, None, None, :] <= pb[None, None, :, None], s, -jnp.inf)
        p = jax.nn.softmax(s, axis=-1)
        return jnp.einsum('bhqk,bkhd->bqhd', p, vf)

    o = lax.map(block, (q_blocks, F_blocks, pos_blocks))
    return o.transpose(1, 0, 2, 3, 4).reshape(bsz, s_len, n_h, hd).astype(q.dtype)


def fox_sample_attention(q, k_new, v_new, logf_new, k_past, v_past, r_past):
    f32 = jnp.float32
    t_len = q.shape[1]
    n_past = k_past.shape[1]
    scale = q.shape[-1] ** -0.5
    Fn = jnp.transpose(jnp.cumsum(logf_new.astype(f32), axis=1), (0, 2, 1))
    qf = q.astype(f32)
    s_past = jnp.einsum('bqhd,bkhd->bhqk', qf, k_past) * scale + Fn[..., None] + r_past[:, :, None, :]
    s_new = jnp.einsum('bqhd,bkhd->bhqk', qf, k_new.astype(f32)) * scale + Fn[..., :, None] - Fn[..., None, :]
    causal = jnp.arange(t_len)[None, :] <= jnp.arange(t_len)[:, None]
    s_new = jnp.where(causal, s_new, -jnp.inf)
    p = jax.nn.softmax(jnp.concatenate([s_past, s_new], axis=-1), axis=-1)
    o = (jnp.einsum('bhqk,bkhd->bqhd', p[..., :n_past], v_past)
         + jnp.einsum('bhqk,bkhd->bqhd', p[..., n_past:], v_new.astype(f32)))
    return o.astype(q.dtype)


def gather_pages(cache, page_table):
    g = cache[page_table]
    return g.reshape((page_table.shape[0], page_table.shape[1] * cache.shape[1]) + cache.shape[2:])


def setup_inputs(seed: int = 0) -> dict:
    key = jax.random.key(seed)
    keys = jax.random.split(key, 64)
    counter = [0]

    def nxt():
        counter[0] += 1
        return keys[counter[0] - 1]

    f32 = jnp.float32

    def nrm(shape, scale):
        return jax.random.normal(nxt(), shape, f32) * scale

    D, H, N, E, F = D_MODEL, N_HEADS, HEAD_DIM, N_EXPERTS, D_EXPERT
    n_pages = PAST_LEN // PAGE_SIZE
    n_used = DEC_BATCH * n_pages
    n_pool = n_used + max(1, n_used // 4)
    perm = jax.random.permutation(nxt(), n_pool)
    page_table = perm[:n_used].reshape(DEC_BATCH, n_pages).astype(jnp.int32)
    sd = D ** -0.5
    return {
        'x_prompt': nrm((BATCH, SEQ, D), 1.0),
        'x_sample': nrm((DEC_BATCH, DEC_SEQ, D), 1.0),
        'c_prompt': nrm((BATCH, D), 1.0),
        'c_sample': nrm((DEC_BATCH, D), 1.0),
        'state_shift': nrm((N_A, DEC_BATCH, D), 1.0),
        'state_wkv': nrm((N_A, DEC_BATCH, H, N, N), 0.3),
        'cache_k': nrm((n_pool, PAGE_SIZE, H, N), 1.0),
        'cache_v': nrm((n_pool, PAGE_SIZE, H, N), 1.0),
        'cache_logf': jax.nn.log_sigmoid(7.0 + nrm((n_pool, PAGE_SIZE, H), 1.0)),
        'page_table': page_table,
        'mod_w': nrm((DEPTH, D, 6 * D), 0.3 * sd),
        'mod_b': nrm((DEPTH, 6 * D), 0.1),
        'norm_g': 1.0 + nrm((DEPTH, 2, D), 0.05),
        'a_mu': jax.random.uniform(nxt(), (N_A, 6, D), f32),
        'a_w0': jax.random.uniform(nxt(), (N_A, D), f32, minval=-6.0, maxval=-1.0),
        'a_w1': nrm((N_A, D, A_DECAY_LORA), sd),
        'a_w2': nrm((N_A, A_DECAY_LORA, D), 0.5 * A_DECAY_LORA ** -0.5),
        'a_a0': nrm((N_A, D), 0.1),
        'a_a1': nrm((N_A, D, A_AAA_LORA), sd),
        'a_a2': nrm((N_A, A_AAA_LORA, D), 0.5 * A_AAA_LORA ** -0.5),
        'a_v0': nrm((N_A - 1, D), 0.1),
        'a_v1': nrm((N_A - 1, D, A_MV_LORA), sd),
        'a_v2': nrm((N_A - 1, A_MV_LORA, D), 0.5 * A_MV_LORA ** -0.5),
        'a_g1': nrm((N_A, D, A_GATE_LORA), sd),
        'a_g2': nrm((N_A, A_GATE_LORA, D), A_GATE_LORA ** -0.5),
        'a_kk': 0.85 + nrm((N_A, D), 0.05),
        'a_ka': 1.0 + nrm((N_A, D), 0.05),
        'a_rk': nrm((N_A, H, N), 0.1),
        'a_wr': nrm((N_A, D, D), sd),
        'a_wk': nrm((N_A, D, D), sd),
        'a_wv': nrm((N_A, D, D), sd),
        'a_wo': nrm((N_A, D, D), sd),
        'a_lnx_w': 1.0 + nrm((N_A, D), 0.05),
        'a_lnx_b': nrm((N_A, D), 0.01),
        'kv_mod_w': nrm((D, 2 * D), 0.3 * sd),
        'kv_mod_b': nrm((2 * D,), 0.1),
        'kv_norm_g': 1.0 + nrm((D,), 0.05),
        'kv_w': nrm((D, 2 * D + H), sd),
        'kv_b_f': 5.0 + nrm((H,), 0.5),
        'k_norm_g': 1.0 + nrm((N,), 0.05),
        'b_wq': nrm((N_B, D, D), sd),
        'b_q_norm_g': 1.0 + nrm((N_B, N), 0.05),
        'b_wo': nrm((N_B, D, D), sd),
        'router_w': nrm((D, E), sd),
        'router_bias': nrm((E,), 0.01),
        'moe_w_gate': nrm((DEPTH, E, D, F), sd),
        'moe_w_up': nrm((DEPTH, E, D, F), sd),
        'moe_w_down': nrm((DEPTH, E, F, D), F ** -0.5),
    }


def reference(x_prompt, x_sample, c_prompt, c_sample, state_shift, state_wkv, cache_k, cache_v,
              cache_logf, page_table, mod_w, mod_b, norm_g, a_mu, a_w0, a_w1, a_w2, a_a0, a_a1,
              a_a2, a_v0, a_v1, a_v2, a_g1, a_g2, a_kk, a_ka, a_rk, a_wr, a_wk, a_wv, a_wo,
              a_lnx_w, a_lnx_b, kv_mod_w, kv_mod_b, kv_norm_g, kv_w, kv_b_f, k_norm_g, b_wq,
              b_q_norm_g, b_wo, router_w, router_bias, moe_w_gate, moe_w_up, moe_w_down):
    f32 = jnp.float32

    def trunk(x, c, shift0, wkv0, attend):
        bsz, t_len, _ = x.shape
        v_first = None
        shifts, states = [], []
        k_sh = v_sh = lf_sh = None
        for i in range(DEPTH):
            sh_m, sc_m, gt_m, sh_f, sc_f, gt_f = adaln_params(c, mod_w[i], mod_b[i], 6)
            xn = modulate(x, norm_g[i, 0], sh_m, sc_m)
            if i < N_A:
                v_mix = None if i == 0 else (a_v0[i - 1], a_v1[i - 1], a_v2[i - 1])
                mixed, v_first, last_row, S = rwkv7_time_mix(
                    xn, shift0[i], wkv0[i], v_first, v_mix, a_mu[i], a_w0[i], a_w1[i], a_w2[i],
                    a_a0[i], a_a1[i], a_a2[i], a_g1[i], a_g2[i], a_kk[i], a_ka[i], a_rk[i],
                    a_wr[i], a_wk[i], a_wv[i], a_wo[i], a_lnx_w[i], a_lnx_b[i])
                shifts.append(last_row)
                states.append(S)
            else:
                j = i - N_A
                if j == 0:
                    k_sh, v_sh, lf_sh = shared_kv(x, c, kv_mod_w, kv_mod_b, kv_norm_g, kv_w, kv_b_f, k_norm_g)
                q = rms_norm((xn @ b_wq[j]).reshape(bsz, t_len, N_HEADS, HEAD_DIM), b_q_norm_g[j])
                mixed = attend(q, k_sh, v_sh, lf_sh).reshape(bsz, t_len, D_MODEL) @ b_wo[j]
            x = x + gt_m[:, None, :] * mixed
            xn = modulate(x, norm_g[i, 1], sh_f, sc_f)
            x = x + gt_f[:, None, :] * moe_ffn(xn, router_w, router_bias, moe_w_gate[i], moe_w_up[i], moe_w_down[i])
        return x, jnp.stack(shifts), jnp.stack(states), k_sh, v_sh, lf_sh

    bp = x_prompt.shape[0]
    shift0_p = jnp.zeros((N_A, bp, D_MODEL), x_prompt.dtype)
    wkv0_p = jnp.zeros((N_A, bp, N_HEADS, HEAD_DIM, HEAD_DIM), f32)
    y_p, shift_p, wkv_p, k_p, v_p, lf_p = trunk(x_prompt, c_prompt, shift0_p, wkv0_p, fox_prompt_attention)

    k_past = gather_pages(cache_k, page_table).astype(f32)
    v_past = gather_pages(cache_v, page_table).astype(f32)
    lf_past = gather_pages(cache_logf, page_table).astype(f32)
    r_past = jnp.transpose(lax.cumsum(lf_past, axis=1, reverse=True) - lf_past, (0, 2, 1))

    def attend_sample(q, k_new, v_new, lf_new):
        return fox_sample_attention(q, k_new, v_new, lf_new, k_past, v_past, r_past)

    y_s, shift_s, wkv_s, k_s, v_s, lf_s = trunk(x_sample, c_sample, state_shift, state_wkv, attend_sample)
    return (y_p, y_s, shift_p, wkv_p, k_p, v_p, lf_p, shift_s, wkv_s, k_s, v_s, lf_s)
```

```python
import functools

import jax
import jax.numpy as jnp
from jax import lax
from jax.experimental import pallas as pl
from jax.experimental.pallas import tpu as pltpu

D_MODEL = 1024
HEAD_DIM = 64
N_HEADS = D_MODEL // HEAD_DIM
N_EXPERTS = 16
N_GROUPS = 4
EXPERTS_PER_GROUP = N_EXPERTS // N_GROUPS
D_EXPERT = 512
DEPTH = 4
N_A = 2
RMS_EPS = 1e-6
GN_EPS = 64e-5
LANES = 128
VMEM_LIMIT = 56 * 1024 * 1024

F32 = jnp.float32
BF16 = jnp.bfloat16
NEG_INF = float("-inf")


def _cparams(*sem):
    return pltpu.CompilerParams(dimension_semantics=sem, vmem_limit_bytes=VMEM_LIMIT)


def _bdot(a, b):
    return jnp.dot(a.astype(BF16), b.astype(BF16), preferred_element_type=F32)


def _split_dot(a, m):
    a1 = a.astype(BF16)
    r1 = a - a1.astype(F32)
    a2 = r1.astype(BF16)
    a3 = (r1 - a2.astype(F32)).astype(BF16)
    return (jnp.dot(a1, m, preferred_element_type=F32) + jnp.dot(a2, m, preferred_element_type=F32)
            + jnp.dot(a3, m, preferred_element_type=F32))


def _sigmoid(z):
    return 1.0 / (1.0 + jnp.exp(-z))


def _silu(z):
    return z * _sigmoid(z)


def _softplus(z):
    return jnp.maximum(z, 0.0) + jnp.log(1.0 + jnp.exp(-jnp.abs(z)))


def _modulate(x, g, sh, sc):
    y = x * lax.rsqrt(jnp.mean(x * x, axis=-1, keepdims=True) + RMS_EPS)
    return (y * g) * (1.0 + sc) + sh


def _head_rms(z, gsum, gexp, g_row):
    ss = _split_dot(z * z, gsum)
    inv = lax.rsqrt(ss * (1.0 / HEAD_DIM) + RMS_EPS)
    return z * _split_dot(inv, gexp) * g_row


def _group_mats():
    head_of_lane = jnp.arange(D_MODEL) // HEAD_DIM
    gsum = (head_of_lane[:, None] == jnp.arange(LANES)[None, :]).astype(BF16)
    return gsum, gsum.T


class _Rows:
    def __init__(self, bsz, t_len, tile):
        self.bsz, self.t_len = bsz, t_len
        self.rows = bsz * t_len
        self.tm = min(tile, self.rows)
        self.long = t_len >= self.tm
        if self.long:
            assert t_len % self.tm == 0
        else:
            assert self.tm % t_len == 0
        assert self.rows % self.tm == 0
        self.n_tiles = self.rows // self.tm
        self.tiles_per_seq = t_len // self.tm if self.long else 1

    def row_spec(self, width=D_MODEL):
        return pl.BlockSpec((self.tm, width), lambda i, *_: (i, 0))

    def mod_operand(self, m):
        w = m.shape[-1]
        if self.long:
            tps = self.tiles_per_seq
            return m.reshape(self.bsz, 1, w), pl.BlockSpec((None, 1, w), lambda i, *_: (i // tps, 0, 0))
        return jnp.repeat(m, self.t_len, axis=0), pl.BlockSpec((self.tm, w), lambda i, *_: (i, 0))


def _full_spec(a):
    nd = a.ndim
    return pl.BlockSpec(a.shape, lambda *_: (0,) * nd)


def _adaln_kernel(c_ref, w_ref, b_ref, o_ref):
    c = c_ref[...]
    o_ref[...] = jnp.dot(_silu(c), w_ref[...], precision=lax.Precision.HIGHEST,
                         preferred_element_type=F32) + b_ref[...]


def adaln_all(c, w, b, tn=1024):
    m, k = c.shape
    n = w.shape[1]
    return pl.pallas_call(
        _adaln_kernel,
        out_shape=jax.ShapeDtypeStruct((m, n), F32),
        grid=(n // tn,),
        in_specs=[pl.BlockSpec((m, k), lambda j: (0, 0)), pl.BlockSpec((k, tn), lambda j: (0, j)),
                  pl.BlockSpec((1, tn), lambda j: (0, j))],
        out_specs=pl.BlockSpec((m, tn), lambda j: (0, j)),
        compiler_params=_cparams("parallel"),
        name="adaln",
    )(c, w, b.reshape(1, n))


def _rwkv_proj_kernel(*refs, rows, has_vmix):
    (x_ref, sh_ref, sc_ref, first_ref, g_ref, mu_ref, w0_ref, a0_ref,
     wr_ref, wk_ref, wv_ref, w1_ref, w2_ref, a1_ref, a2_ref, g1_ref, g2_ref) = refs[:17]
    pos = 17
    if has_vmix:
        vf_ref, v0_ref, v1_ref, v2_ref = refs[pos:pos + 4]
        pos += 4
    r_ref, w_ref, k_ref, v_ref, a_ref, gg_ref, last_ref = refs[pos:pos + 7]
    carry = refs[pos + 7]
    tm = rows.tm
    i = pl.program_id(0)

    xn = _modulate(x_ref[...], g_ref[...], sh_ref[...], sc_ref[...])
    rolled = pltpu.roll(xn, 1, 0)
    row = lax.broadcasted_iota(jnp.int32, (tm, 1), 0)
    if rows.long:
        @pl.when(i % rows.tiles_per_seq == 0)
        def _():
            carry[...] = first_ref[...]
        x_prev = jnp.where(row == 0, carry[...], rolled)
        carry[...] = xn[tm - 1:tm, :]
        last_ref[...] = xn[tm - 1:tm, :]
    else:
        x_prev = jnp.where(row % rows.t_len == 0, first_ref[...], rolled)
        last_ref[...] = xn
    xx = x_prev - xn
    mu = mu_ref[...]

    def mix(j):
        return (xn + xx * mu[j:j + 1, :]).astype(BF16)

    xv = mix(3)
    r_ref[...] = _bdot(mix(0), wr_ref[...])
    w_ref[...] = -_softplus(-(w0_ref[...] + _bdot(jnp.tanh(_bdot(mix(1), w1_ref[...])), w2_ref[...]))) - 0.5
    k_ref[...] = _bdot(mix(2), wk_ref[...])
    v = _bdot(xv, wv_ref[...])
    if has_vmix:
        v = v + (vf_ref[...] - v) * _sigmoid(v0_ref[...] + _bdot(_bdot(xv, v1_ref[...]), v2_ref[...]))
    v_ref[...] = v
    a_ref[...] = _sigmoid(a0_ref[...] + _bdot(_bdot(mix(4), a1_ref[...]), a2_ref[...]))
    gg_ref[...] = _bdot(_sigmoid(_bdot(mix(5), g1_ref[...])), g2_ref[...])


def rwkv_proj(rows, x, sh, sc, shift0, norm_g, mu, w0, a0, wr, wk, wv, w1, w2, a1, a2, g1, g2, vmix):
    d = D_MODEL
    sh_op, sh_spec = rows.mod_operand(sh)
    sc_op, sc_spec = rows.mod_operand(sc)
    first_op, first_spec = rows.mod_operand(shift0)
    row2 = lambda z: z.reshape(1, -1)
    ops = [x, sh_op, sc_op, first_op, row2(norm_g), mu, row2(w0), row2(a0),
           wr.astype(BF16), wk.astype(BF16), wv.astype(BF16), w1.astype(BF16), w2.astype(BF16),
           a1.astype(BF16), a2.astype(BF16), g1.astype(BF16), g2.astype(BF16)]
    specs = [rows.row_spec(), sh_spec, sc_spec, first_spec] + [_full_spec(o) for o in ops[4:]]
    if vmix is not None:
        v_first, v0, v1, v2 = vmix
        extra = [v_first, row2(v0), v1.astype(BF16), v2.astype(BF16)]
        ops += extra
        specs += [rows.row_spec()] + [_full_spec(o) for o in extra[1:]]
    act = jax.ShapeDtypeStruct((rows.rows, d), F32)
    if rows.long:
        tps = rows.tiles_per_seq
        last_shape = jax.ShapeDtypeStruct((rows.bsz, 1, d), F32)
        last_spec = pl.BlockSpec((None, 1, d), lambda i: (i // tps, 0, 0))
    else:
        last_shape, last_spec = act, rows.row_spec()
    outs = pl.pallas_call(
        functools.partial(_rwkv_proj_kernel, rows=rows, has_vmix=vmix is not None),
        out_shape=(act,) * 6 + (last_shape,),
        grid=(rows.n_tiles,),
        in_specs=specs,
        out_specs=(rows.row_spec(),) * 6 + (last_spec,),
        scratch_shapes=[pltpu.VMEM((1, d), F32)],
        compiler_params=_cparams("arbitrary"),
        name="rwkv_proj",
    )(*ops)
    last = outs[6]
    last = last.reshape(rows.bsz, d) if rows.long else last.reshape(rows.bsz, rows.t_len, d)[:, -1]
    return outs[:6] + (last,)


def _wkv_kernel(r_ref, w_ref, k_ref, v_ref, a_ref, kk_ref, ka_ref, rk_ref, lw_ref, lb_ref, s0_ref,
                y_ref, st_ref, S, nk_s, bb_s, km_s, dc_s, y_s, *, tt):
    tb = pl.program_id(1)

    @pl.when(tb == 0)
    def _():
        S[...] = s0_ref[...]

    k = k_ref[...]
    a = a_ref[...]
    kkr = k * kk_ref[...][None]
    ss = jnp.sum(kkr * kkr, axis=1, keepdims=True)
    kk = kkr * lax.rsqrt(jnp.maximum(ss, 1e-24))
    km = k * (1.0 + (a - 1.0) * ka_ref[...][None])
    nk_s[...] = -kk
    bb_s[...] = kk * a
    km_s[...] = km
    dc_s[...] = jnp.exp(-jnp.exp(w_ref[...]))

    def step(t, carry):
        def row(ref, j):
            return jnp.broadcast_to(ref[t, pl.ds(j, 1), :], (HEAD_DIM, LANES))

        sa = S[0] * row(nk_s, 0)
        for j in range(1, HEAD_DIM):
            sa = sa + S[j] * row(nk_s, j)
        vt = v_ref[t]
        y = None
        for j in range(HEAD_DIM):
            s_new = S[j] * row(dc_s, j) + sa * row(bb_s, j) + vt * row(km_s, j)
            S[j] = s_new
            yj = s_new * row(r_ref, j)
            y = yj if y is None else y + yj
        y_s[t] = y
        return carry

    lax.fori_loop(0, tt, step, 0)

    y = y_s[...]
    mean = jnp.mean(y, axis=1, keepdims=True)
    yc = y - mean
    var = jnp.mean(yc * yc, axis=1, keepdims=True)
    bonus = jnp.sum(r_ref[...] * km * rk_ref[...][None], axis=1, keepdims=True) * v_ref[...]
    y_ref[...] = yc * lax.rsqrt(var + GN_EPS) * lw_ref[...][None] + lb_ref[...][None] + bonus

    @pl.when(tb == pl.num_programs(1) - 1)
    def _():
        st_ref[...] = S[...]


def wkv_scan(r, w, k, v, a, kk_p, ka_p, rk_p, lw_p, lb_p, s0, *, tt=16):
    t_len, n, l_tot = r.shape
    tt = min(tt, t_len)
    assert t_len % tt == 0 and l_tot % LANES == 0
    seq_spec = pl.BlockSpec((tt, n, LANES), lambda g, t: (t, 0, g))
    par_spec = pl.BlockSpec((n, LANES), lambda g, t: (0, g))
    st_spec = pl.BlockSpec((n, n, LANES), lambda g, t: (0, 0, g))
    return pl.pallas_call(
        functools.partial(_wkv_kernel, tt=tt),
        out_shape=(jax.ShapeDtypeStruct(r.shape, F32), jax.ShapeDtypeStruct(s0.shape, F32)),
        grid=(l_tot // LANES, t_len // tt),
        in_specs=[seq_spec] * 5 + [par_spec] * 5 + [st_spec],
        out_specs=(seq_spec, st_spec),
        scratch_shapes=[pltpu.VMEM((n, n, LANES), F32)] + [pltpu.VMEM((tt, n, LANES), F32)] * 5,
        compiler_params=_cparams("parallel", "arbitrary"),
        name="wkv_scan",
    )(r, w, k, v, a, kk_p, ka_p, rk_p, lw_p, lb_p, s0)


def _to_lanes(z, bsz, t_len):
    return jnp.transpose(z.reshape(bsz, t_len, N_HEADS, HEAD_DIM), (1, 3, 0, 2)).reshape(
        t_len, HEAD_DIM, bsz * N_HEADS)


def _from_lanes(z, bsz, t_len):
    return jnp.transpose(z.reshape(t_len, HEAD_DIM, bsz, N_HEADS), (2, 0, 3, 1)).reshape(
        bsz * t_len, D_MODEL)


def _param_lanes(p, bsz):
    return jnp.broadcast_to(p.reshape(1, N_HEADS, HEAD_DIM), (bsz, N_HEADS, HEAD_DIM)).transpose(
        2, 0, 1).reshape(HEAD_DIM, bsz * N_HEADS)


def _resid_mm_kernel(*refs, has_mul):
    if has_mul:
        a_ref, b_ref, w_ref, x_ref, gt_ref, o_ref = refs
        lhs = a_ref[...] * b_ref[...]
    else:
        a_ref, w_ref, x_ref, gt_ref, o_ref = refs
        lhs = a_ref[...]
    o_ref[...] = x_ref[...] + gt_ref[...] * _bdot(lhs, w_ref[...])


def resid_mm(rows, a, b, w, x, gate):
    gt_op, gt_spec = rows.mod_operand(gate)
    w = w.astype(BF16)
    ops = [a] + ([b] if b is not None else []) + [w, x, gt_op]
    specs = [rows.row_spec()] * (2 if b is not None else 1) + [_full_spec(w), rows.row_spec(), gt_spec]
    return pl.pallas_call(
        functools.partial(_resid_mm_kernel, has_mul=b is not None),
        out_shape=jax.ShapeDtypeStruct(x.shape, F32),
        grid=(rows.n_tiles,),
        in_specs=specs,
        out_specs=rows.row_spec(),
        compiler_params=_cparams("parallel"),
        name="resid_mm",
    )(*ops)


def _route(probs, bias):
    tm = probs.shape[0]
    sel = probs + bias
    lane = lax.broadcasted_iota(jnp.int32, (tm, N_EXPERTS), 1)
    grp = lane // EXPERTS_PER_GROUP

    def top2(mask):
        x = jnp.where(mask, sel, NEG_INF)
        v1 = jnp.max(x, axis=-1, keepdims=True)
        i1 = jnp.min(jnp.where(x == v1, lane, N_EXPERTS), axis=-1, keepdims=True)
        x2 = jnp.where(lane == i1, NEG_INF, x)
        v2 = jnp.max(x2, axis=-1, keepdims=True)
        i2 = jnp.min(jnp.where(x2 == v2, lane, N_EXPERTS), axis=-1, keepdims=True)
        return v1 + v2, i1, i2

    best_score, _, _ = top2(grp == 0)
    best = jnp.zeros((tm, 1), jnp.int32)
    for g in range(1, N_GROUPS):
        score, _, _ = top2(grp == g)
        better = score > best_score
        best = jnp.where(better, g, best)
        best_score = jnp.where(better, score, best_score)
    _, i1, i2 = top2(grp == best)
    chosen = (lane == i1) | (lane == i2)
    w = jnp.where(chosen, probs, 0.0)
    return w / jnp.sum(w, axis=-1, keepdims=True)


def _moe_kernel(x_ref, sh_ref, sc_ref, gt_ref, g_ref, rw_ref, rb_ref, wg_ref, wu_ref, wd_ref,
                o_ref, xn_s, gate_s, acc_s):
    e = pl.program_id(1)

    @pl.when(e == 0)
    def _():
        xn = _modulate(x_ref[...], g_ref[...], sh_ref[...], sc_ref[...])
        xn_s[...] = xn.astype(BF16)
        logits = jnp.dot(xn, rw_ref[...], precision=lax.Precision.HIGHEST, preferred_element_type=F32)
        z = jnp.exp(logits - jnp.max(logits, axis=-1, keepdims=True))
        probs = z / jnp.sum(z, axis=-1, keepdims=True)
        gate_s[...] = _route(probs, rb_ref[...])
        acc_s[...] = jnp.zeros_like(acc_s)

    xn = xn_s[...]
    lane = lax.broadcasted_iota(jnp.int32, gate_s.shape, 1)
    g_e = jnp.sum(jnp.where(lane == e, gate_s[...], 0.0), axis=-1, keepdims=True)
    h = _silu(jnp.dot(xn, wg_ref[...], preferred_element_type=F32)) * jnp.dot(
        xn, wu_ref[...], preferred_element_type=F32)
    acc_s[...] += _bdot(h * g_e, wd_ref[...])

    @pl.when(e == pl.num_programs(1) - 1)
    def _():
        o_ref[...] = x_ref[...] + gt_ref[...] * acc_s[...]


def moe_layer(rows, x, sh, sc, gate, norm_g, router_w, router_bias, w_gate, w_up, w_down):
    d = D_MODEL
    sh_op, sh_spec = rows.mod_operand(sh)
    sc_op, sc_spec = rows.mod_operand(sc)
    gt_op, gt_spec = rows.mod_operand(gate)
    g2 = norm_g.reshape(1, d)
    rb = router_bias.reshape(1, N_EXPERTS)
    tm = rows.tm
    return pl.pallas_call(
        _moe_kernel,
        out_shape=jax.ShapeDtypeStruct(x.shape, F32),
        grid=(rows.n_tiles, N_EXPERTS),
        in_specs=[rows.row_spec(), sh_spec, sc_spec, gt_spec, _full_spec(g2), _full_spec(router_w),
                  _full_spec(rb),
                  pl.BlockSpec((None, d, D_EXPERT), lambda i, e: (e, 0, 0)),
                  pl.BlockSpec((None, d, D_EXPERT), lambda i, e: (e, 0, 0)),
                  pl.BlockSpec((None, D_EXPERT, d), lambda i, e: (e, 0, 0))],
        out_specs=rows.row_spec(),
        scratch_shapes=[pltpu.VMEM((tm, d), BF16), pltpu.VMEM((tm, N_EXPERTS), F32),
                        pltpu.VMEM((tm, d), F32)],
        compiler_params=_cparams("parallel", "arbitrary"),
        name="moe",
    )(x, sh_op, sc_op, gt_op, g2, router_w, rb, w_gate, w_up, w_down)


def _kv_proj_kernel(x_ref, sh_ref, sc_ref, g_ref, wk_ref, wv_ref, wf_ref, bf_ref, kg_ref, gsum_ref,
                    gexp_ref, k_ref, v_ref, lf_ref, fc_ref, carry, *, rows):
    tm = rows.tm
    i = pl.program_id(0)
    xn = _modulate(x_ref[...], g_ref[...], sh_ref[...], sc_ref[...]).astype(BF16)
    k_ref[...] = _head_rms(jnp.dot(xn, wk_ref[...], preferred_element_type=F32), gsum_ref[...],
                           gexp_ref[...], kg_ref[...])
    v_ref[...] = jnp.dot(xn, wv_ref[...], preferred_element_type=F32)
    z = jnp.dot(xn, wf_ref[...], preferred_element_type=F32) + bf_ref[...]
    lf = -_softplus(-z)
    lf_ref[...] = lf[:, :N_HEADS]
    ri = lax.broadcasted_iota(jnp.int32, (tm, tm), 0)
    ci = lax.broadcasted_iota(jnp.int32, (tm, tm), 1)
    if rows.long:
        tri = (ci <= ri).astype(BF16)

        @pl.when(i % rows.tiles_per_seq == 0)
        def _():
            carry[...] = jnp.zeros_like(carry)
        cum = _split_dot_lhs_exact(tri, lf) + carry[...]
        carry[...] = cum[tm - 1:tm, :]
    else:
        tri = ((ci <= ri) & (ci // rows.t_len == ri // rows.t_len)).astype(BF16)
        cum = _split_dot_lhs_exact(tri, lf)
    fc_ref[...] = cum[:, :N_HEADS]


def _split_dot_lhs_exact(m, b):
    b1 = b.astype(BF16)
    r1 = b - b1.astype(F32)
    b2 = r1.astype(BF16)
    b3 = (r1 - b2.astype(F32)).astype(BF16)
    return (jnp.dot(m, b1, preferred_element_type=F32) + jnp.dot(m, b2, preferred_element_type=F32)
            + jnp.dot(m, b3, preferred_element_type=F32))


def kv_proj(rows, x, sh, sc, norm_g, kv_w, kv_b_f, k_norm_g):
    d = D_MODEL
    sh_op, sh_spec = rows.mod_operand(sh)
    sc_op, sc_spec = rows.mod_operand(sc)
    gsum, gexp = _group_mats()
    wk = kv_w[:, :d].astype(BF16)
    wv = kv_w[:, d:2 * d].astype(BF16)
    wf = jnp.pad(kv_w[:, 2 * d:], ((0, 0), (0, LANES - N_HEADS))).astype(BF16)
    bf = jnp.pad(kv_b_f, (0, LANES - N_HEADS)).reshape(1, LANES)
    kg = jnp.tile(k_norm_g, N_HEADS).reshape(1, d)
    consts = [norm_g.reshape(1, d), wk, wv, wf, bf, kg, gsum, gexp]
    act = jax.ShapeDtypeStruct((rows.rows, d), F32)
    small = jax.ShapeDtypeStruct((rows.rows, N_HEADS), F32)
    return pl.pallas_call(
        functools.partial(_kv_proj_kernel, rows=rows),
        out_shape=(act, act, small, small),
        grid=(rows.n_tiles,),
        in_specs=[rows.row_spec(), sh_spec, sc_spec] + [_full_spec(c) for c in consts],
        out_specs=(rows.row_spec(), rows.row_spec(), rows.row_spec(N_HEADS), rows.row_spec(N_HEADS)),
        scratch_shapes=[pltpu.VMEM((1, LANES), F32)],
        compiler_params=_cparams("arbitrary"),
        name="kv_proj",
    )(x, sh_op, sc_op, *consts)


def _q_proj_kernel(x_ref, sh_ref, sc_ref, g_ref, wq_ref, qg_ref, gsum_ref, gexp_ref, q_ref):
    xn = _modulate(x_ref[...], g_ref[...], sh_ref[...], sc_ref[...])
    q = _head_rms(_bdot(xn, wq_ref[...]), gsum_ref[...], gexp_ref[...], qg_ref[...])
    q_ref[...] = q * (HEAD_DIM ** -0.5)


def q_proj(rows, x, sh, sc, norm_g, wq, q_norm_g):
    d = D_MODEL
    sh_op, sh_spec = rows.mod_operand(sh)
    sc_op, sc_spec = rows.mod_operand(sc)
    gsum, gexp = _group_mats()
    consts = [norm_g.reshape(1, d), wq.astype(BF16), jnp.tile(q_norm_g, N_HEADS).reshape(1, d), gsum, gexp]
    return pl.pallas_call(
        _q_proj_kernel,
        out_shape=jax.ShapeDtypeStruct((rows.rows, d), F32),
        grid=(rows.n_tiles,),
        in_specs=[rows.row_spec(), sh_spec, sc_spec] + [_full_spec(c) for c in consts],
        out_specs=rows.row_spec(),
        compiler_params=_cparams("parallel"),
        name="q_proj",
    )(x, sh_op, sc_op, *consts)


def _fox_prompt_kernel(q_ref, k_ref, v_ref, fc_ref, fr_ref, o_ref, m_s, l_s, acc_s, *, tq, tk, nk):
    hp = pl.program_id(1)
    qi = pl.program_id(2)
    lane = lax.broadcasted_iota(jnp.int32, (1, LANES), 1)
    q = q_ref[...]
    fc_all = fc_ref[...]
    lane16 = lax.broadcasted_iota(jnp.int32, fc_all.shape, 1)
    row_pos = qi * tq + lax.broadcasted_iota(jnp.int32, (tq, tk), 0)
    col_in = lax.broadcasted_iota(jnp.int32, (tq, tk), 1)
    qms, fcs = [], []
    for hh in range(2):
        head = 2 * hp + hh
        qms.append(jnp.where((lane // HEAD_DIM) == hh, q, 0.0).astype(BF16))
        fcs.append(jnp.sum(jnp.where(lane16 == head, fc_all, 0.0), axis=-1, keepdims=True))
        m_s[hh] = jnp.full((tq, 1), NEG_INF, F32)
        l_s[hh] = jnp.zeros((tq, 1), F32)
        acc_s[hh] = jnp.zeros((tq, LANES), F32)

    n_steps = (qi * tq) // tk + 1

    def body(ki, c):
        start = pl.multiple_of(ki * tk, tk)
        kt = k_ref[pl.ds(start, tk), :].astype(BF16)
        vt = v_ref[pl.ds(start, tk), :].astype(BF16)
        causal = (start + col_in) <= row_pos
        for hh in range(2):
            head = 2 * hp + hh
            fr = fr_ref[pl.ds(head * nk + ki, 1), :]
            s = lax.dot_general(qms[hh], kt, (((1,), (1,)), ((), ())), preferred_element_type=F32)
            s = jnp.where(causal, s + fcs[hh] - fr, NEG_INF)
            m_old = m_s[hh]
            m_new = jnp.maximum(m_old, jnp.max(s, axis=-1, keepdims=True))
            p = jnp.exp(s - m_new)
            alpha = jnp.exp(m_old - m_new)
            l_s[hh] = alpha * l_s[hh] + jnp.sum(p, axis=-1, keepdims=True)
            acc_s[hh] = alpha * acc_s[hh] + jnp.dot(p.astype(BF16), vt, preferred_element_type=F32)
            m_s[hh] = m_new
        return c

    lax.fori_loop(0, n_steps, body, 0)
    o0 = acc_s[0] / l_s[0]
    o1 = acc_s[1] / l_s[1]
    o_ref[...] = jnp.where((lane // HEAD_DIM) == 0, o0, o1)


def fox_prompt_attention(q, k, v, fcum, bsz, t_len, tq=256, tk=256):
    d = D_MODEL
    tq, tk = min(tq, t_len), min(tk, t_len)
    nk = t_len // tk
    q3, k3, v3 = (z.reshape(bsz, t_len, d) for z in (q, k, v))
    fc3 = fcum.reshape(bsz, t_len, N_HEADS)
    fr3 = jnp.transpose(fc3, (0, 2, 1)).reshape(bsz, N_HEADS * nk, tk)
    out = pl.pallas_call(
        functools.partial(_fox_prompt_kernel, tq=tq, tk=tk, nk=nk),
        out_shape=jax.ShapeDtypeStruct((bsz, t_len, d), F32),
        grid=(bsz, N_HEADS // 2, t_len // tq),
        in_specs=[pl.BlockSpec((None, tq, LANES), lambda b, h, i: (b, i, h)),
                  pl.BlockSpec((None, t_len, LANES), lambda b, h, i: (b, 0, h)),
                  pl.BlockSpec((None, t_len, LANES), lambda b, h, i: (b, 0, h)),
                  pl.BlockSpec((None, tq, N_HEADS), lambda b, h, i: (b, i, 0)),
                  pl.BlockSpec((None, N_HEADS * nk, tk), lambda b, h, i: (b, 0, 0))],
        out_specs=pl.BlockSpec((None, tq, LANES), lambda b, h, i: (b, i, h)),
        scratch_shapes=[pltpu.VMEM((2, tq, 1), F32), pltpu.VMEM((2, tq, 1), F32),
                        pltpu.VMEM((2, tq, LANES), F32)],
        compiler_params=_cparams("parallel", "parallel", "arbitrary"),
        name="fox_prompt",
    )(q3, k3, v3, fc3, fr3)
    return out.reshape(bsz * t_len, d)


def _fox_sample_kernel(pt_ref, q_ref, kn_ref, vn_ref, fcol_ref, fnr_ref, ck_ref, cv_ref, clf_ref, o_ref,
                       qbd_s, m_s, l_s, acc_s, rc_s, *, t_len, page):
    p = pl.program_id(1)
    nrow = t_len * N_HEADS
    head_of_row = lax.broadcasted_iota(jnp.int32, (nrow, D_MODEL), 0) % N_HEADS
    head_of_col = lax.broadcasted_iota(jnp.int32, (nrow, D_MODEL), 1) // HEAD_DIM
    diag = head_of_row == head_of_col
    fcol = fcol_ref[...]

    @pl.when(p == 0)
    def _():
        q = q_ref[...]
        qrep = jnp.broadcast_to(q[:, None, :], (t_len, N_HEADS, D_MODEL)).reshape(nrow, D_MODEL)
        qbd = jnp.where(diag, qrep, 0.0)
        qbd_s[...] = qbd.astype(BF16)
        q_of_row = lax.broadcasted_iota(jnp.int32, (nrow, 1), 0) // N_HEADS
        fnr = fnr_ref[...]
        s_cols = []
        for t in range(t_len):
            s_t = jnp.sum(qbd * kn_ref[pl.ds(t, 1), :], axis=-1, keepdims=True)
            s_cols.append(jnp.where(q_of_row >= t, s_t + fcol - fnr[:, t:t + 1], NEG_INF))
        m = functools.reduce(jnp.maximum, s_cols)
        l = jnp.zeros((nrow, 1), F32)
        acc = jnp.zeros((nrow, D_MODEL), F32)
        for t in range(t_len):
            p_t = jnp.exp(s_cols[t] - m)
            l = l + p_t
            acc = acc + p_t * vn_ref[pl.ds(t, 1), :]
        m_s[...] = m
        l_s[...] = l
        acc_s[...] = acc
        rc_s[...] = jnp.zeros_like(rc_s)

    lf = clf_ref[...]
    ji = lax.broadcasted_iota(jnp.int32, (page, page), 0)
    si = lax.broadcasted_iota(jnp.int32, (page, page), 1)
    upper = (ji > si).astype(BF16)
    r16 = _split_dot(lf, upper) + rc_s[...]
    rc_s[...] = rc_s[...] + jnp.sum(lf, axis=-1, keepdims=True)
    r_rows = jnp.concatenate([r16] * t_len, axis=0)

    kt = ck_ref[...].astype(BF16)
    s = lax.dot_general(qbd_s[...], kt, (((1,), (1,)), ((), ())), preferred_element_type=F32)
    s = s + fcol + r_rows
    m_old = m_s[...]
    m_new = jnp.maximum(m_old, jnp.max(s, axis=-1, keepdims=True))
    pr = jnp.exp(s - m_new)
    alpha = jnp.exp(m_old - m_new)
    l_s[...] = alpha * l_s[...] + jnp.sum(pr, axis=-1, keepdims=True)
    acc_s[...] = alpha * acc_s[...] + jnp.dot(pr.astype(BF16), cv_ref[...].astype(BF16),
                                              preferred_element_type=F32)
    m_s[...] = m_new

    @pl.when(p == pl.num_programs(1) - 1)
    def _():
        o = jnp.where(diag, acc_s[...] / l_s[...], 0.0)
        o_ref[...] = jnp.sum(o.reshape(t_len, N_HEADS, D_MODEL), axis=1)


def fox_sample_attention(q, k_new, v_new, fcum, cache_k, cache_v, cache_logf, page_table, bsz, t_len):
    d = D_MODEL
    n_pool, page = cache_k.shape[:2]
    n_pages = page_table.shape[1]
    nrow = t_len * N_HEADS
    q3, kn3, vn3 = (z.reshape(bsz, t_len, d) for z in (q, k_new, v_new))
    fc3 = fcum.reshape(bsz, t_len, N_HEADS)
    fcol = fc3.reshape(bsz, nrow, 1)
    fnr = jnp.broadcast_to(jnp.transpose(fc3, (0, 2, 1))[:, None], (bsz, t_len, N_HEADS, t_len)).reshape(
        bsz, nrow, t_len)
    ck = cache_k.reshape(n_pool, page, d)
    cv = cache_v.reshape(n_pool, page, d)
    clf = jnp.transpose(cache_logf, (0, 2, 1))
    per_b = lambda shape: pl.BlockSpec((None,) + shape, lambda b, p, pt: (b, 0, 0))
    paged = lambda shape: pl.BlockSpec((None,) + shape, lambda b, p, pt: (pt[b, n_pages - 1 - p], 0, 0))
    out = pl.pallas_call(
        functools.partial(_fox_sample_kernel, t_len=t_len, page=page),
        out_shape=jax.ShapeDtypeStruct((bsz, t_len, d), F32),
        grid_spec=pltpu.PrefetchScalarGridSpec(
            num_scalar_prefetch=1,
            grid=(bsz, n_pages),
            in_specs=[per_b((t_len, d)), per_b((t_len, d)), per_b((t_len, d)), per_b((nrow, 1)),
                      per_b((nrow, t_len)), paged((page, d)), paged((page, d)), paged((N_HEADS, page))],
            out_specs=per_b((t_len, d)),
            scratch_shapes=[pltpu.VMEM((nrow, d), BF16), pltpu.VMEM((nrow, 1), F32),
                            pltpu.VMEM((nrow, 1), F32), pltpu.VMEM((nrow, d), F32),
                            pltpu.VMEM((N_HEADS, 1), F32)]),
        compiler_params=_cparams("parallel", "arbitrary"),
        name="fox_sample",
    )(page_table, q3, kn3, vn3, fcol, fnr, ck, cv, clf)
    return out.reshape(bsz * t_len, d)


def _trunk(x3, mods, kv_mods, shift0, wkv0, attend, p, moe_w):
    bsz, t_len, d = x3.shape
    rows = _Rows(bsz, t_len, 256)
    rows_moe = _Rows(bsz, t_len, 512)
    x = x3.reshape(bsz * t_len, d)
    shifts, states = [], []
    v_first = None
    k_sh = v_sh = lf_sh = fcum = None
    for i in range(DEPTH):
        sh_m, sc_m, gt_m, sh_f, sc_f, gt_f = jnp.split(mods[i], 6, axis=-1)
        if i < N_A:
            vmix = None if i == 0 else (v_first, p['a_v0'][i - 1], p['a_v1'][i - 1], p['a_v2'][i - 1])
            r, w, k, v, a, g, last = rwkv_proj(
                rows, x, sh_m, sc_m, shift0[i], p['norm_g'][i, 0], p['a_mu'][i], p['a_w0'][i], p['a_a0'][i],
                p['a_wr'][i], p['a_wk'][i], p['a_wv'][i], p['a_w1'][i], p['a_w2'][i], p['a_a1'][i],
                p['a_a2'][i], p['a_g1'][i], p['a_g2'][i], vmix)
            if i == 0:
                v_first = v
            shifts.append(last)
            lanes = [_to_lanes(z, bsz, t_len) for z in (r, w, k, v, a)]
            pars = [_param_lanes(z, bsz) for z in (p['a_kk'][i], p['a_ka'][i], p['a_rk'][i].reshape(-1),
                                                   p['a_lnx_w'][i], p['a_lnx_b'][i])]
            s0 = jnp.transpose(wkv0[i], (3, 2, 0, 1)).reshape(HEAD_DIM, HEAD_DIM, bsz * N_HEADS)
            y, s_t = wkv_scan(*lanes, *pars, s0)
            states.append(jnp.transpose(s_t.reshape(HEAD_DIM, HEAD_DIM, bsz, N_HEADS), (2, 3, 1, 0)))
            x = resid_mm(rows, _from_lanes(y, bsz, t_len), g, p['a_wo'][i], x, gt_m)
        else:
            j = i - N_A
            if j == 0:
                k_sh, v_sh, lf_sh, fcum = kv_proj(rows, x, kv_mods[0], kv_mods[1], p['kv_norm_g'], p['kv_w'],
                                                  p['kv_b_f'], p['k_norm_g'])
            q = q_proj(rows, x, sh_m, sc_m, p['norm_g'][i, 0], p['b_wq'][j], p['b_q_norm_g'][j])
            o = attend(q, k_sh, v_sh, fcum)
            x = resid_mm(rows, o, None, p['b_wo'][j], x, gt_m)
        x = moe_layer(rows_moe, x, sh_f, sc_f, gt_f, p['norm_g'][i, 1], p['router_w'], p['router_bias'],
                      moe_w[0][i], moe_w[1][i], moe_w[2][i])
    hd = (bsz, t_len, N_HEADS, HEAD_DIM)
    return (x.reshape(bsz, t_len, d), jnp.stack(shifts), jnp.stack(states), k_sh.reshape(hd),
            v_sh.reshape(hd), lf_sh.reshape(bsz, t_len, N_HEADS))


def kernel(x_prompt, x_sample, c_prompt, c_sample, state_shift, state_wkv, cache_k, cache_v, cache_logf, page_table, mod_w, mod_b, norm_g, a_mu, a_w0, a_w1, a_w2, a_a0, a_a1, a_a2, a_v0, a_v1, a_v2, a_g1, a_g2, a_kk, a_ka, a_rk, a_wr, a_wk, a_wv, a_wo, a_lnx_w, a_lnx_b, kv_mod_w, kv_mod_b, kv_norm_g, kv_w, kv_b_f, k_norm_g, b_wq, b_q_norm_g, b_wo, router_w, router_bias, moe_w_gate, moe_w_up, moe_w_down):
    p = dict(norm_g=norm_g, a_mu=a_mu, a_w0=a_w0, a_w1=a_w1, a_w2=a_w2, a_a0=a_a0, a_a1=a_a1, a_a2=a_a2,
             a_v0=a_v0, a_v1=a_v1, a_v2=a_v2, a_g1=a_g1, a_g2=a_g2, a_kk=a_kk, a_ka=a_ka, a_rk=a_rk,
             a_wr=a_wr, a_wk=a_wk, a_wv=a_wv, a_wo=a_wo, a_lnx_w=a_lnx_w, a_lnx_b=a_lnx_b,
             kv_norm_g=kv_norm_g, kv_w=kv_w, kv_b_f=kv_b_f, k_norm_g=k_norm_g, b_wq=b_wq,
             b_q_norm_g=b_q_norm_g, b_wo=b_wo, router_w=router_w, router_bias=router_bias)
    d = D_MODEL
    bp, t_p = x_prompt.shape[:2]
    bs, t_s = x_sample.shape[:2]
    moe_w = (moe_w_gate.astype(BF16), moe_w_up.astype(BF16), moe_w_down.astype(BF16))

    w_all = jnp.concatenate([mod_w[i] for i in range(DEPTH)] + [kv_mod_w], axis=1)
    b_all = jnp.concatenate([mod_b[i] for i in range(DEPTH)] + [kv_mod_b], axis=0)
    m_all = adaln_all(jnp.concatenate([c_prompt, c_sample], axis=0), w_all, b_all)

    def split_mods(m):
        layer = [m[:, i * 6 * d:(i + 1) * 6 * d] for i in range(DEPTH)]
        kvm = m[:, DEPTH * 6 * d:]
        return layer, (kvm[:, :d], kvm[:, d:])

    mods_p, kvm_p = split_mods(m_all[:bp])
    mods_s, kvm_s = split_mods(m_all[bp:])

    def attend_prompt(q, k, v, fcum):
        return fox_prompt_attention(q, k, v, fcum, bp, t_p)

    def attend_sample(q, k, v, fcum):
        return fox_sample_attention(q, k, v, fcum, cache_k, cache_v, cache_logf, page_table, bs, t_s)

    shift0_p = jnp.zeros((N_A, bp, d), F32)
    wkv0_p = jnp.zeros((N_A, bp, N_HEADS, HEAD_DIM, HEAD_DIM), F32)
    y_p, shift_p, wkv_p, k_p, v_p, lf_p = _trunk(x_prompt, mods_p, kvm_p, shift0_p, wkv0_p, attend_prompt,
                                                 p, moe_w)
    y_s, shift_s, wkv_s, k_s, v_s, lf_s = _trunk(x_sample, mods_s, kvm_s, state_shift, state_wkv,
                                                 attend_sample, p, moe_w)
    return (y_p, y_s, shift_p, wkv_p, k_p, v_p, lf_p, shift_s, wkv_s, k_s, v_s, lf_s)
```

```python
import functools

import jax
import jax.numpy as jnp
from jax import lax
from jax.experimental import pallas as pl
from jax.experimental.pallas import tpu as pltpu

D_MODEL = 1024
HEAD_DIM = 64
N_HEADS = D_MODEL // HEAD_DIM
N_EXPERTS = 16
N_GROUPS = 4
EXPERTS_PER_GROUP = N_EXPERTS // N_GROUPS
D_EXPERT = 512
DEPTH = 4
N_A = 2
RMS_EPS = 1e-6
GN_EPS = 64e-5
LANES = 128
VMEM_LIMIT = 56 * 1024 * 1024
SPARSE_MOE_MIN_ROWS = 4096

F32 = jnp.float32
BF16 = jnp.bfloat16
NEG_INF = float("-inf")


def _cparams(*sem):
    return pltpu.CompilerParams(dimension_semantics=sem, vmem_limit_bytes=VMEM_LIMIT)


def _bdot(a, b):
    return jnp.dot(a.astype(BF16), b.astype(BF16), preferred_element_type=F32)


def _split_dot(a, m):
    a1 = a.astype(BF16)
    r1 = a - a1.astype(F32)
    a2 = r1.astype(BF16)
    a3 = (r1 - a2.astype(F32)).astype(BF16)
    return (jnp.dot(a1, m, preferred_element_type=F32) + jnp.dot(a2, m, preferred_element_type=F32)
            + jnp.dot(a3, m, preferred_element_type=F32))


def _sigmoid(z):
    return 1.0 / (1.0 + jnp.exp(-z))


def _silu(z):
    return z * _sigmoid(z)


def _softplus(z):
    return jnp.maximum(z, 0.0) + jnp.log(1.0 + jnp.exp(-jnp.abs(z)))


def _modulate(x, g, sh, sc):
    y = x * lax.rsqrt(jnp.mean(x * x, axis=-1, keepdims=True) + RMS_EPS)
    return (y * g) * (1.0 + sc) + sh


def _head_rms(z, gsum, gexp, g_row):
    ss = _split_dot(z * z, gsum)
    inv = lax.rsqrt(ss * (1.0 / HEAD_DIM) + RMS_EPS)
    return z * _split_dot(inv, gexp) * g_row


def _group_mats():
    head_of_lane = jnp.arange(D_MODEL) // HEAD_DIM
    gsum = (head_of_lane[:, None] == jnp.arange(LANES)[None, :]).astype(BF16)
    return gsum, gsum.T


class _Rows:
    def __init__(self, bsz, t_len, tile):
        self.bsz, self.t_len = bsz, t_len
        self.rows = bsz * t_len
        self.tm = min(tile, self.rows)
        self.long = t_len >= self.tm
        if self.long:
            assert t_len % self.tm == 0
        else:
            assert self.tm % t_len == 0
        assert self.rows % self.tm == 0
        self.n_tiles = self.rows // self.tm
        self.tiles_per_seq = t_len // self.tm if self.long else 1

    def row_spec(self, width=D_MODEL):
        return pl.BlockSpec((self.tm, width), lambda i, *_: (i, 0))

    def mod_operand(self, m):
        w = m.shape[-1]
        if self.long:
            tps = self.tiles_per_seq
            return m.reshape(self.bsz, 1, w), pl.BlockSpec((None, 1, w), lambda i, *_: (i // tps, 0, 0))
        return jnp.repeat(m, self.t_len, axis=0), pl.BlockSpec((self.tm, w), lambda i, *_: (i, 0))


def _full_spec(a):
    nd = a.ndim
    return pl.BlockSpec(a.shape, lambda *_: (0,) * nd)


def _adaln_kernel(c_ref, w_ref, b_ref, o_ref):
    c = c_ref[...]
    o_ref[...] = jnp.dot(_silu(c), w_ref[...], precision=lax.Precision.HIGHEST,
                         preferred_element_type=F32) + b_ref[...]


def adaln_all(c, w, b, tn=1024):
    m, k = c.shape
    n = w.shape[1]
    return pl.pallas_call(
        _adaln_kernel,
        out_shape=jax.ShapeDtypeStruct((m, n), F32),
        grid=(n // tn,),
        in_specs=[pl.BlockSpec((m, k), lambda j: (0, 0)), pl.BlockSpec((k, tn), lambda j: (0, j)),
                  pl.BlockSpec((1, tn), lambda j: (0, j))],
        out_specs=pl.BlockSpec((m, tn), lambda j: (0, j)),
        compiler_params=_cparams("parallel"),
        name="adaln",
    )(c, w, b.reshape(1, n))


def _rwkv_proj_kernel(*refs, rows, has_vmix):
    (x_ref, sh_ref, sc_ref, first_ref, g_ref, mu_ref, w0_ref, a0_ref,
     wr_ref, wk_ref, wv_ref, w1_ref, w2_ref, a1_ref, a2_ref, g1_ref, g2_ref) = refs[:17]
    pos = 17
    if has_vmix:
        vf_ref, v0_ref, v1_ref, v2_ref = refs[pos:pos + 4]
        pos += 4
    r_ref, w_ref, k_ref, v_ref, a_ref, gg_ref, last_ref = refs[pos:pos + 7]
    carry = refs[pos + 7]
    tm = rows.tm
    i = pl.program_id(0)

    xn = _modulate(x_ref[...], g_ref[...], sh_ref[...], sc_ref[...])
    rolled = pltpu.roll(xn, 1, 0)
    row = lax.broadcasted_iota(jnp.int32, (tm, 1), 0)
    if rows.long:
        @pl.when(i % rows.tiles_per_seq == 0)
        def _():
            carry[...] = first_ref[...]
        x_prev = jnp.where(row == 0, carry[...], rolled)
        carry[...] = xn[tm - 1:tm, :]
        last_ref[...] = xn[tm - 1:tm, :]
    else:
        x_prev = jnp.where(row % rows.t_len == 0, first_ref[...], rolled)
        last_ref[...] = xn
    xx = x_prev - xn
    mu = mu_ref[...]

    def mix(j):
        return (xn + xx * mu[j:j + 1, :]).astype(BF16)

    xv = mix(3)
    r_ref[...] = _bdot(mix(0), wr_ref[...])
    w_ref[...] = -_softplus(-(w0_ref[...] + _bdot(jnp.tanh(_bdot(mix(1), w1_ref[...])), w2_ref[...]))) - 0.5
    k_ref[...] = _bdot(mix(2), wk_ref[...])
    v = _bdot(xv, wv_ref[...])
    if has_vmix:
        v = v + (vf_ref[...] - v) * _sigmoid(v0_ref[...] + _bdot(_bdot(xv, v1_ref[...]), v2_ref[...]))
    v_ref[...] = v
    a_ref[...] = _sigmoid(a0_ref[...] + _bdot(_bdot(mix(4), a1_ref[...]), a2_ref[...]))
    gg_ref[...] = _bdot(_sigmoid(_bdot(mix(5), g1_ref[...])), g2_ref[...])


def rwkv_proj(rows, x, sh, sc, shift0, norm_g, mu, w0, a0, wr, wk, wv, w1, w2, a1, a2, g1, g2, vmix):
    d = D_MODEL
    sh_op, sh_spec = rows.mod_operand(sh)
    sc_op, sc_spec = rows.mod_operand(sc)
    first_op, first_spec = rows.mod_operand(shift0)
    row2 = lambda z: z.reshape(1, -1)
    ops = [x, sh_op, sc_op, first_op, row2(norm_g), mu, row2(w0), row2(a0),
           wr.astype(BF16), wk.astype(BF16), wv.astype(BF16), w1.astype(BF16), w2.astype(BF16),
           a1.astype(BF16), a2.astype(BF16), g1.astype(BF16), g2.astype(BF16)]
    specs = [rows.row_spec(), sh_spec, sc_spec, first_spec] + [_full_spec(o) for o in ops[4:]]
    if vmix is not None:
        v_first, v0, v1, v2 = vmix
        extra = [v_first, row2(v0), v1.astype(BF16), v2.astype(BF16)]
        ops += extra
        specs += [rows.row_spec()] + [_full_spec(o) for o in extra[1:]]
    act = jax.ShapeDtypeStruct((rows.rows, d), F32)
    if rows.long:
        tps = rows.tiles_per_seq
        last_shape = jax.ShapeDtypeStruct((rows.bsz, 1, d), F32)
        last_spec = pl.BlockSpec((None, 1, d), lambda i: (i // tps, 0, 0))
    else:
        last_shape, last_spec = act, rows.row_spec()
    outs = pl.pallas_call(
        functools.partial(_rwkv_proj_kernel, rows=rows, has_vmix=vmix is not None),
        out_shape=(act,) * 6 + (last_shape,),
        grid=(rows.n_tiles,),
        in_specs=specs,
        out_specs=(rows.row_spec(),) * 6 + (last_spec,),
        scratch_shapes=[pltpu.VMEM((1, d), F32)],
        compiler_params=_cparams("arbitrary"),
        name="rwkv_proj",
    )(*ops)
    last = outs[6]
    last = last.reshape(rows.bsz, d) if rows.long else last.reshape(rows.bsz, rows.t_len, d)[:, -1]
    return outs[:6] + (last,)


def _wkv_kernel(r_ref, w_ref, k_ref, v_ref, a_ref, kk_ref, ka_ref, rk_ref, lw_ref, lb_ref, s0_ref,
                y_ref, st_ref, S, nk_s, bb_s, km_s, dc_s, y_s, *, tt):
    tb = pl.program_id(1)

    @pl.when(tb == 0)
    def _():
        S[...] = s0_ref[...]

    k = k_ref[...]
    a = a_ref[...]
    kkr = k * kk_ref[...][None]
    ss = jnp.sum(kkr * kkr, axis=1, keepdims=True)
    kk = kkr * lax.rsqrt(jnp.maximum(ss, 1e-24))
    km = k * (1.0 + (a - 1.0) * ka_ref[...][None])
    nk_s[...] = -kk
    bb_s[...] = kk * a
    km_s[...] = km
    dc_s[...] = jnp.exp(-jnp.exp(w_ref[...]))

    def step(t, carry):
        def row(ref, j):
            return jnp.broadcast_to(ref[t, pl.ds(j, 1), :], (HEAD_DIM, LANES))

        sa = S[0] * row(nk_s, 0)
        for j in range(1, HEAD_DIM):
            sa = sa + S[j] * row(nk_s, j)
        vt = v_ref[t]
        y = None
        for j in range(HEAD_DIM):
            s_new = S[j] * row(dc_s, j) + sa * row(bb_s, j) + vt * row(km_s, j)
            S[j] = s_new
            yj = s_new * row(r_ref, j)
            y = yj if y is None else y + yj
        y_s[t] = y
        return carry

    lax.fori_loop(0, tt, step, 0)

    y = y_s[...]
    mean = jnp.mean(y, axis=1, keepdims=True)
    yc = y - mean
    var = jnp.mean(yc * yc, axis=1, keepdims=True)
    bonus = jnp.sum(r_ref[...] * km * rk_ref[...][None], axis=1, keepdims=True) * v_ref[...]
    y_ref[...] = yc * lax.rsqrt(var + GN_EPS) * lw_ref[...][None] + lb_ref[...][None] + bonus

    @pl.when(tb == pl.num_programs(1) - 1)
    def _():
        st_ref[...] = S[...]


def wkv_scan(r, w, k, v, a, kk_p, ka_p, rk_p, lw_p, lb_p, s0, *, tt=16):
    t_len, n, l_tot = r.shape
    tt = min(tt, t_len)
    assert t_len % tt == 0 and l_tot % LANES == 0
    seq_spec = pl.BlockSpec((tt, n, LANES), lambda g, t: (t, 0, g))
    par_spec = pl.BlockSpec((n, LANES), lambda g, t: (0, g))
    st_spec = pl.BlockSpec((n, n, LANES), lambda g, t: (0, 0, g))
    return pl.pallas_call(
        functools.partial(_wkv_kernel, tt=tt),
        out_shape=(jax.ShapeDtypeStruct(r.shape, F32), jax.ShapeDtypeStruct(s0.shape, F32)),
        grid=(l_tot // LANES, t_len // tt),
        in_specs=[seq_spec] * 5 + [par_spec] * 5 + [st_spec],
        out_specs=(seq_spec, st_spec),
        scratch_shapes=[pltpu.VMEM((n, n, LANES), F32)] + [pltpu.VMEM((tt, n, LANES), F32)] * 5,
        compiler_params=_cparams("parallel", "arbitrary"),
        name="wkv_scan",
    )(r, w, k, v, a, kk_p, ka_p, rk_p, lw_p, lb_p, s0)


def _to_lanes(z, bsz, t_len):
    return jnp.transpose(z.reshape(bsz, t_len, N_HEADS, HEAD_DIM), (1, 3, 0, 2)).reshape(
        t_len, HEAD_DIM, bsz * N_HEADS)


def _from_lanes(z, bsz, t_len):
    return jnp.transpose(z.reshape(t_len, HEAD_DIM, bsz, N_HEADS), (2, 0, 3, 1)).reshape(
        bsz * t_len, D_MODEL)


def _param_lanes(p, bsz):
    return jnp.broadcast_to(p.reshape(1, N_HEADS, HEAD_DIM), (bsz, N_HEADS, HEAD_DIM)).transpose(
        2, 0, 1).reshape(HEAD_DIM, bsz * N_HEADS)


def _resid_mm_kernel(*refs, has_mul):
    if has_mul:
        a_ref, b_ref, w_ref, x_ref, gt_ref, o_ref = refs
        lhs = a_ref[...] * b_ref[...]
    else:
        a_ref, w_ref, x_ref, gt_ref, o_ref = refs
        lhs = a_ref[...]
    o_ref[...] = x_ref[...] + gt_ref[...] * _bdot(lhs, w_ref[...])


def resid_mm(rows, a, b, w, x, gate):
    gt_op, gt_spec = rows.mod_operand(gate)
    w = w.astype(BF16)
    ops = [a] + ([b] if b is not None else []) + [w, x, gt_op]
    specs = [rows.row_spec()] * (2 if b is not None else 1) + [_full_spec(w), rows.row_spec(), gt_spec]
    return pl.pallas_call(
        functools.partial(_resid_mm_kernel, has_mul=b is not None),
        out_shape=jax.ShapeDtypeStruct(x.shape, F32),
        grid=(rows.n_tiles,),
        in_specs=specs,
        out_specs=rows.row_spec(),
        compiler_params=_cparams("parallel"),
        name="resid_mm",
    )(*ops)


def _route(probs, bias):
    tm = probs.shape[0]
    sel = probs + bias
    lane = lax.broadcasted_iota(jnp.int32, (tm, N_EXPERTS), 1)
    grp = lane // EXPERTS_PER_GROUP

    def top2(mask):
        x = jnp.where(mask, sel, NEG_INF)
        v1 = jnp.max(x, axis=-1, keepdims=True)
        i1 = jnp.min(jnp.where(x == v1, lane, N_EXPERTS), axis=-1, keepdims=True)
        x2 = jnp.where(lane == i1, NEG_INF, x)
        v2 = jnp.max(x2, axis=-1, keepdims=True)
        i2 = jnp.min(jnp.where(x2 == v2, lane, N_EXPERTS), axis=-1, keepdims=True)
        return v1 + v2, i1, i2

    best_score, _, _ = top2(grp == 0)
    best = jnp.zeros((tm, 1), jnp.int32)
    for g in range(1, N_GROUPS):
        score, _, _ = top2(grp == g)
        better = score > best_score
        best = jnp.where(better, g, best)
        best_score = jnp.where(better, score, best_score)
    _, i1, i2 = top2(grp == best)
    chosen = (lane == i1) | (lane == i2)
    w = jnp.where(chosen, probs, 0.0)
    return w / jnp.sum(w, axis=-1, keepdims=True), best


def _router_gate(xn, rw, rb):
    logits = jnp.dot(xn, rw, precision=lax.Precision.HIGHEST, preferred_element_type=F32)
    z = jnp.exp(logits - jnp.max(logits, axis=-1, keepdims=True))
    return _route(z / jnp.sum(z, axis=-1, keepdims=True), rb)


def _moe_kernel(x_ref, sh_ref, sc_ref, gt_ref, g_ref, rw_ref, rb_ref, wg_ref, wu_ref, wd_ref,
                o_ref, xn_s, gate_s, acc_s):
    e = pl.program_id(1)

    @pl.when(e == 0)
    def _():
        xn = _modulate(x_ref[...], g_ref[...], sh_ref[...], sc_ref[...])
        xn_s[...] = xn.astype(BF16)
        gate_s[...], _ = _router_gate(xn, rw_ref[...], rb_ref[...])
        acc_s[...] = jnp.zeros_like(acc_s)

    acc_s[...] += _expert_ffn(xn_s[...], gate_s[...], e, wg_ref, wu_ref, wd_ref)

    @pl.when(e == pl.num_programs(1) - 1)
    def _():
        o_ref[...] = x_ref[...] + gt_ref[...] * acc_s[...]


def _expert_ffn(xn_bf, gate, expert, wg_ref, wu_ref, wd_ref):
    lane = lax.broadcasted_iota(jnp.int32, gate.shape, 1)
    g_e = jnp.sum(jnp.where(lane == expert, gate, 0.0), axis=-1, keepdims=True)
    h = _silu(jnp.dot(xn_bf, wg_ref[...], preferred_element_type=F32)) * jnp.dot(
        xn_bf, wu_ref[...], preferred_element_type=F32)
    return _bdot(h * g_e, wd_ref[...])


def moe_layer(rows, x, sh, sc, gate, norm_g, router_w, router_bias, w_gate, w_up, w_down):
    d = D_MODEL
    sh_op, sh_spec = rows.mod_operand(sh)
    sc_op, sc_spec = rows.mod_operand(sc)
    gt_op, gt_spec = rows.mod_operand(gate)
    g2 = norm_g.reshape(1, d)
    rb = router_bias.reshape(1, N_EXPERTS)
    tm = rows.tm
    return pl.pallas_call(
        _moe_kernel,
        out_shape=jax.ShapeDtypeStruct(x.shape, F32),
        grid=(rows.n_tiles, N_EXPERTS),
        in_specs=[rows.row_spec(), sh_spec, sc_spec, gt_spec, _full_spec(g2), _full_spec(router_w),
                  _full_spec(rb),
                  pl.BlockSpec((None, d, D_EXPERT), lambda i, e: (e, 0, 0)),
                  pl.BlockSpec((None, d, D_EXPERT), lambda i, e: (e, 0, 0)),
                  pl.BlockSpec((None, D_EXPERT, d), lambda i, e: (e, 0, 0))],
        out_specs=rows.row_spec(),
        scratch_shapes=[pltpu.VMEM((tm, d), BF16), pltpu.VMEM((tm, N_EXPERTS), F32),
                        pltpu.VMEM((tm, d), F32)],
        compiler_params=_cparams("parallel", "arbitrary"),
        name="moe",
    )(x, sh_op, sc_op, gt_op, g2, router_w, rb, w_gate, w_up, w_down)


PAYLOAD = D_MODEL + LANES


def _moe_route_kernel(x_ref, sh_ref, sc_ref, g_ref, rw_ref, rb_ref, xg_ref, grp_ref):
    xn = _modulate(x_ref[...], g_ref[...], sh_ref[...], sc_ref[...])
    gate, best = _router_gate(xn, rw_ref[...], rb_ref[...])
    xg_ref[:, :D_MODEL] = xn
    xg_ref[:, D_MODEL:] = jnp.zeros((xn.shape[0], LANES), F32)
    xg_ref[:, D_MODEL:D_MODEL + N_EXPERTS] = gate
    grp_ref[...] = best


def moe_route(rows, x, sh, sc, norm_g, router_w, router_bias):
    d = D_MODEL
    sh_op, sh_spec = rows.mod_operand(sh)
    sc_op, sc_spec = rows.mod_operand(sc)
    g2 = norm_g.reshape(1, d)
    rb = router_bias.reshape(1, N_EXPERTS)
    return pl.pallas_call(
        _moe_route_kernel,
        out_shape=(jax.ShapeDtypeStruct((rows.rows, PAYLOAD), F32),
                   jax.ShapeDtypeStruct((rows.rows, 1), jnp.int32)),
        grid=(rows.n_tiles,),
        in_specs=[rows.row_spec(), sh_spec, sc_spec, _full_spec(g2), _full_spec(router_w), _full_spec(rb)],
        out_specs=(rows.row_spec(PAYLOAD), rows.row_spec(1)),
        compiler_params=_cparams("parallel"),
        name="moe_route",
    )(x, sh_op, sc_op, g2, router_w, rb)


def _group_order(grp, tm):
    n = grp.shape[0]
    p_slots = n + N_GROUPS * tm
    onehot = (grp[:, None] == jnp.arange(N_GROUPS, dtype=jnp.int32)[None, :]).astype(jnp.int32)
    counts = jnp.sum(onehot, axis=0)
    rank = jnp.sum((jnp.cumsum(onehot, axis=0) - onehot) * onehot, axis=1)
    padded = ((counts + tm - 1) // tm) * tm
    ends = jnp.cumsum(padded)
    pos = jnp.sum(onehot * (ends - padded)[None, :], axis=1) + rank
    token = jnp.arange(n, dtype=jnp.int32)
    src = jnp.zeros((p_slots + 2 * tm,), jnp.int32).at[pos].set(token)
    real = jnp.zeros((p_slots,), jnp.int32).at[pos].set(1)
    dst = jnp.where(real == 1, src[:p_slots], n + jnp.cumsum(1 - real) - 1).astype(jnp.int32)
    dst_prev = jnp.concatenate([p_slots + jnp.arange(tm, dtype=jnp.int32), dst])
    tile_start = jnp.arange(p_slots // tm + 1, dtype=jnp.int32) * tm
    tile_grp = jnp.sum((tile_start[:, None] >= ends[None, :]).astype(jnp.int32), axis=1)
    tile_grp = jnp.where(tile_grp < N_GROUPS, tile_grp, -1).astype(jnp.int32)
    return src, dst_prev, tile_grp


def _moe_sparse_kernel(src_ref, dst_ref, tg_ref, xg_hbm, wg_ref, wu_ref, wd_ref, y_hbm,
                       xbuf, ybuf, xn_s, gate_s, acc_s, gsem, ssem, *, tm):
    i = pl.program_id(0)
    e = pl.program_id(1)
    nt = pl.num_programs(0)
    slot = i % 2
    prev = 1 - slot
    quarter = tm // EXPERTS_PER_GROUP

    def row_in(tile, to_slot, k):
        return pltpu.make_async_copy(xg_hbm.at[pl.ds(src_ref[tile * tm + k], 1), :],
                                     xbuf.at[to_slot, pl.ds(k, 1), :], gsem.at[to_slot])

    def row_out(tile, from_slot, k):
        return pltpu.make_async_copy(ybuf.at[from_slot, pl.ds(k, 1), :],
                                     y_hbm.at[pl.ds(dst_ref[tile * tm + k], 1), :], ssem.at[from_slot])

    def wait_in(slot_):
        pltpu.make_async_copy(xg_hbm.at[pl.ds(0, tm), :], xbuf.at[slot_], gsem.at[slot_]).wait()

    def wait_out(slot_):
        pltpu.make_async_copy(ybuf.at[slot_], y_hbm.at[pl.ds(0, tm), :], ssem.at[slot_]).wait()

    @pl.when((i == 0) & (e == 0))
    def _():
        def body(k, c):
            row_in(0, 0, k).start()
            return c
        lax.fori_loop(0, tm, body, 0)
        ybuf[1] = jnp.zeros((tm, D_MODEL), F32)

    @pl.when(e == 0)
    def _():
        wait_in(slot)
        xn_s[...] = xbuf[slot, :, :D_MODEL].astype(BF16)
        gate_s[...] = xbuf[slot, :, D_MODEL:]

    def start_neighbour_rows():
        for kk in range(quarter):
            k = e * quarter + kk
            row_in(i + 1, prev, k).start()
            row_out(i, prev, k).start()

    grp = tg_ref[i]

    @pl.when(grp >= 0)
    def _():
        start_neighbour_rows()
        contrib = _expert_ffn(xn_s[...], gate_s[...], grp * EXPERTS_PER_GROUP + e, wg_ref, wu_ref, wd_ref)

        @pl.when(e == 0)
        def _():
            acc_s[...] = contrib

        @pl.when(e > 0)
        def _():
            acc_s[...] += contrib

    @pl.when(grp < 0)
    def _():
        start_neighbour_rows()

    @pl.when(e == EXPERTS_PER_GROUP - 1)
    def _():
        @pl.when(i >= 1)
        def _():
            wait_out(slot)

        @pl.when(grp >= 0)
        def _():
            ybuf[slot] = acc_s[...]

        @pl.when(i == nt - 1)
        def _():
            wait_out(prev)
            wait_in(prev)


def moe_sparse(xg, grp, w_gate, w_up, w_down, tm=512):
    d = D_MODEL
    src, dst_prev, tile_grp = _group_order(grp, tm)
    n_tiles = tile_grp.shape[0]
    expert_of = lambda i, e, s, t, tg: (jnp.maximum(tg[i], 0) * EXPERTS_PER_GROUP + e, 0, 0)
    return pl.pallas_call(
        functools.partial(_moe_sparse_kernel, tm=tm),
        out_shape=jax.ShapeDtypeStruct((n_tiles * tm, d), F32),
        grid_spec=pltpu.PrefetchScalarGridSpec(
            num_scalar_prefetch=3,
            grid=(n_tiles, EXPERTS_PER_GROUP),
            in_specs=[pl.BlockSpec(memory_space=pl.ANY),
                      pl.BlockSpec((None, d, D_EXPERT), expert_of),
                      pl.BlockSpec((None, d, D_EXPERT), expert_of),
                      pl.BlockSpec((None, D_EXPERT, d), expert_of)],
            out_specs=pl.BlockSpec(memory_space=pl.ANY),
            scratch_shapes=[pltpu.VMEM((2, tm, PAYLOAD), F32), pltpu.VMEM((2, tm, d), F32),
                            pltpu.VMEM((tm, d), BF16), pltpu.VMEM((tm, LANES), F32),
                            pltpu.VMEM((tm, d), F32),
                            pltpu.SemaphoreType.DMA((2,)), pltpu.SemaphoreType.DMA((2,))]),
        compiler_params=_cparams("arbitrary", "arbitrary"),
        name="moe_sparse",
    )(src, dst_prev, tile_grp, xg, w_gate, w_up, w_down)


def _resid_add_kernel(x_ref, y_ref, gt_ref, o_ref):
    o_ref[...] = x_ref[...] + gt_ref[...] * y_ref[...]


def resid_add(rows, x, y, gate):
    gt_op, gt_spec = rows.mod_operand(gate)
    return pl.pallas_call(
        _resid_add_kernel,
        out_shape=jax.ShapeDtypeStruct(x.shape, F32),
        grid=(rows.n_tiles,),
        in_specs=[rows.row_spec(), rows.row_spec(), gt_spec],
        out_specs=rows.row_spec(),
        compiler_params=_cparams("parallel"),
        name="resid_add",
    )(x, y, gt_op)


def _kv_proj_kernel(x_ref, sh_ref, sc_ref, g_ref, wk_ref, wv_ref, wf_ref, bf_ref, kg_ref, gsum_ref,
                    gexp_ref, k_ref, v_ref, lf_ref, fc_ref, carry, *, rows):
    tm = rows.tm
    i = pl.program_id(0)
    xn = _modulate(x_ref[...], g_ref[...], sh_ref[...], sc_ref[...]).astype(BF16)
    k_ref[...] = _head_rms(jnp.dot(xn, wk_ref[...], preferred_element_type=F32), gsum_ref[...],
                           gexp_ref[...], kg_ref[...])
    v_ref[...] = jnp.dot(xn, wv_ref[...], preferred_element_type=F32)
    z = jnp.dot(xn, wf_ref[...], preferred_element_type=F32) + bf_ref[...]
    lf = -_softplus(-z)
    lf_ref[...] = lf[:, :N_HEADS]
    ri = lax.broadcasted_iota(jnp.int32, (tm, tm), 0)
    ci = lax.broadcasted_iota(jnp.int32, (tm, tm), 1)
    if rows.long:
        tri = (ci <= ri).astype(BF16)

        @pl.when(i % rows.tiles_per_seq == 0)
        def _():
            carry[...] = jnp.zeros_like(carry)
        cum = _split_dot_lhs_exact(tri, lf) + carry[...]
        carry[...] = cum[tm - 1:tm, :]
    else:
        tri = ((ci <= ri) & (ci // rows.t_len == ri // rows.t_len)).astype(BF16)
        cum = _split_dot_lhs_exact(tri, lf)
    fc_ref[...] = cum[:, :N_HEADS]


def _split_dot_lhs_exact(m, b):
    b1 = b.astype(BF16)
    r1 = b - b1.astype(F32)
    b2 = r1.astype(BF16)
    b3 = (r1 - b2.astype(F32)).astype(BF16)
    return (jnp.dot(m, b1, preferred_element_type=F32) + jnp.dot(m, b2, preferred_element_type=F32)
            + jnp.dot(m, b3, preferred_element_type=F32))


def kv_proj(rows, x, sh, sc, norm_g, kv_w, kv_b_f, k_norm_g):
    d = D_MODEL
    sh_op, sh_spec = rows.mod_operand(sh)
    sc_op, sc_spec = rows.mod_operand(sc)
    gsum, gexp = _group_mats()
    wk = kv_w[:, :d].astype(BF16)
    wv = kv_w[:, d:2 * d].astype(BF16)
    wf = jnp.pad(kv_w[:, 2 * d:], ((0, 0), (0, LANES - N_HEADS))).astype(BF16)
    bf = jnp.pad(kv_b_f, (0, LANES - N_HEADS)).reshape(1, LANES)
    kg = jnp.tile(k_norm_g, N_HEADS).reshape(1, d)
    consts = [norm_g.reshape(1, d), wk, wv, wf, bf, kg, gsum, gexp]
    act = jax.ShapeDtypeStruct((rows.rows, d), F32)
    small = jax.ShapeDtypeStruct((rows.rows, N_HEADS), F32)
    return pl.pallas_call(
        functools.partial(_kv_proj_kernel, rows=rows),
        out_shape=(act, act, small, small),
        grid=(rows.n_tiles,),
        in_specs=[rows.row_spec(), sh_spec, sc_spec] + [_full_spec(c) for c in consts],
        out_specs=(rows.row_spec(), rows.row_spec(), rows.row_spec(N_HEADS), rows.row_spec(N_HEADS)),
        scratch_shapes=[pltpu.VMEM((1, LANES), F32)],
        compiler_params=_cparams("arbitrary"),
        name="kv_proj",
    )(x, sh_op, sc_op, *consts)


def _q_proj_kernel(x_ref, sh_ref, sc_ref, g_ref, wq_ref, qg_ref, gsum_ref, gexp_ref, q_ref):
    xn = _modulate(x_ref[...], g_ref[...], sh_ref[...], sc_ref[...])
    q = _head_rms(_bdot(xn, wq_ref[...]), gsum_ref[...], gexp_ref[...], qg_ref[...])
    q_ref[...] = q * (HEAD_DIM ** -0.5)


def q_proj(rows, x, sh, sc, norm_g, wq, q_norm_g):
    d = D_MODEL
    sh_op, sh_spec = rows.mod_operand(sh)
    sc_op, sc_spec = rows.mod_operand(sc)
    gsum, gexp = _group_mats()
    consts = [norm_g.reshape(1, d), wq.astype(BF16), jnp.tile(q_norm_g, N_HEADS).reshape(1, d), gsum, gexp]
    return pl.pallas_call(
        _q_proj_kernel,
        out_shape=jax.ShapeDtypeStruct((rows.rows, d), F32),
        grid=(rows.n_tiles,),
        in_specs=[rows.row_spec(), sh_spec, sc_spec] + [_full_spec(c) for c in consts],
        out_specs=rows.row_spec(),
        compiler_params=_cparams("parallel"),
        name="q_proj",
    )(x, sh_op, sc_op, *consts)


def _fox_prompt_kernel(q_ref, k_ref, v_ref, f_ref, o_ref, m_s, l_s, acc_s, *, tile):
    hp = pl.program_id(1)
    qi = pl.program_id(2)
    lane = lax.broadcasted_iota(jnp.int32, (1, LANES), 1)
    q = q_ref[...]
    qms = []
    for hh in range(2):
        qms.append(jnp.where((lane // HEAD_DIM) == hh, q, 0.0).astype(BF16))
        m_s[hh] = jnp.full((1, tile), NEG_INF, F32)
        l_s[hh] = jnp.zeros((1, tile), F32)
        acc_s[hh] = jnp.zeros((LANES, tile), F32)

    def step(ki, diagonal):
        start = pl.multiple_of(ki * tile, tile)
        kt = k_ref[pl.ds(start, tile), :].astype(BF16)
        vt = v_ref[pl.ds(start, tile), :].astype(BF16)
        fk = f_ref[pl.ds(start, tile), :]
        lane_h = lax.broadcasted_iota(jnp.int32, fk.shape, 1)
        for hh in range(2):
            f_key = jnp.sum(jnp.where(lane_h == 2 * hp + hh, fk, 0.0), axis=-1, keepdims=True)
            s = lax.dot_general(kt, qms[hh], (((1,), (1,)), ((), ())), preferred_element_type=F32) - f_key
            if diagonal:
                key = lax.broadcasted_iota(jnp.int32, (tile, tile), 0)
                qry = lax.broadcasted_iota(jnp.int32, (tile, tile), 1)
                s = jnp.where(key <= qry, s, NEG_INF)
            m_old = m_s[hh]
            m_new = jnp.maximum(m_old, jnp.max(s, axis=0, keepdims=True))
            p = jnp.exp(s - m_new)
            alpha = jnp.exp(m_old - m_new)
            l_s[hh] = alpha * l_s[hh] + jnp.sum(p, axis=0, keepdims=True)
            acc_s[hh] = alpha * acc_s[hh] + lax.dot_general(
                vt, p.astype(BF16), (((0,), (0,)), ((), ())), preferred_element_type=F32)
            m_s[hh] = m_new

    def body(ki, c):
        step(ki, False)
        return c

    lax.fori_loop(0, qi, body, 0)
    step(qi, True)
    chan = lax.broadcasted_iota(jnp.int32, (LANES, 1), 0)
    o_t = jnp.where((chan // HEAD_DIM) == 0, acc_s[0] / l_s[0], acc_s[1] / l_s[1])
    o_ref[...] = o_t.T


def fox_prompt_attention(q, k, v, fcum, bsz, t_len, tile=512):
    d = D_MODEL
    tile = min(tile, t_len)
    q3, k3, v3 = (z.reshape(bsz, t_len, d) for z in (q, k, v))
    f3 = fcum.reshape(bsz, t_len, N_HEADS)
    out = pl.pallas_call(
        functools.partial(_fox_prompt_kernel, tile=tile),
        out_shape=jax.ShapeDtypeStruct((bsz, t_len, d), F32),
        grid=(bsz, N_HEADS // 2, t_len // tile),
        in_specs=[pl.BlockSpec((None, tile, LANES), lambda b, h, i: (b, i, h)),
                  pl.BlockSpec((None, t_len, LANES), lambda b, h, i: (b, 0, h)),
                  pl.BlockSpec((None, t_len, LANES), lambda b, h, i: (b, 0, h)),
                  pl.BlockSpec((None, t_len, N_HEADS), lambda b, h, i: (b, 0, 0))],
        out_specs=pl.BlockSpec((None, tile, LANES), lambda b, h, i: (b, i, h)),
        scratch_shapes=[pltpu.VMEM((2, 1, tile), F32), pltpu.VMEM((2, 1, tile), F32),
                        pltpu.VMEM((2, LANES, tile), F32)],
        compiler_params=_cparams("parallel", "parallel", "arbitrary"),
        name="fox_prompt",
    )(q3, k3, v3, f3)
    return out.reshape(bsz * t_len, d)


def _fox_sample_kernel(pt_ref, qt_ref, kn_ref, vn_ref, fnb_ref, *rest, t_len, page, pps):
    page_refs = rest[:3 * pps]
    o_ref, m_s, l_s, acc_s, rc_s = rest[3 * pps:]
    p = pl.program_id(1)
    ncol = t_len * N_HEADS
    qt = qt_ref[...]

    def own_head(nj):
        hrow = lax.broadcasted_iota(jnp.int32, (nj, N_HEADS, ncol), 1)
        hcol = lax.broadcasted_iota(jnp.int32, (nj, N_HEADS, ncol), 2) % N_HEADS
        return hrow == hcol

    def scores(k_ref, nj):
        kf = k_ref[...].reshape(nj * N_HEADS, HEAD_DIM).astype(BF16)
        return jnp.dot(kf, qt, preferred_element_type=F32).reshape(nj, N_HEADS, ncol)

    def accumulate(parts):
        m_old = m_s[...]
        m_new = m_old
        for s, _, _ in parts:
            m_new = jnp.maximum(m_new, jnp.max(jnp.max(s, axis=0), axis=0, keepdims=True))
        alpha = jnp.exp(m_old - m_new)
        l = alpha * l_s[...]
        acc = alpha * acc_s[...]
        for s, v_ref, nj in parts:
            pr = jnp.exp(s - m_new[None])
            l = l + jnp.sum(jnp.sum(pr, axis=0), axis=0, keepdims=True)
            vf = v_ref[...].reshape(nj * N_HEADS, HEAD_DIM).astype(BF16)
            acc = acc + lax.dot_general(vf, pr.reshape(nj * N_HEADS, ncol).astype(BF16),
                                        (((0,), (0,)), ((), ())), preferred_element_type=F32)
        m_s[...] = m_new
        l_s[...] = l
        acc_s[...] = acc

    @pl.when(p == 0)
    def _():
        m_s[...] = jnp.full_like(m_s, NEG_INF)
        l_s[...] = jnp.zeros_like(l_s)
        acc_s[...] = jnp.zeros_like(acc_s)
        rc_s[...] = jnp.zeros_like(rc_s)
        t_key = lax.broadcasted_iota(jnp.int32, (t_len, N_HEADS, ncol), 0)
        t_qry = lax.broadcasted_iota(jnp.int32, (t_len, N_HEADS, ncol), 2) // N_HEADS
        s = scores(kn_ref, t_len) - fnb_ref[...]
        s = jnp.where(own_head(t_len) & (t_key <= t_qry), s, NEG_INF)
        accumulate([(s, vn_ref, t_len)])

    ji = lax.broadcasted_iota(jnp.int32, (page, page), 0)
    si = lax.broadcasted_iota(jnp.int32, (page, page), 1)
    upper = (si > ji).astype(BF16)
    own = own_head(page)
    rc = rc_s[...]
    parts = []
    for i in range(pps):
        k_ref, v_ref, lf_ref = page_refs[3 * i:3 * i + 3]
        lf = jnp.concatenate([lf_ref[...]] * t_len, axis=1)
        r = _split_dot_lhs_exact(upper, lf) + rc
        rc = rc + jnp.sum(lf, axis=0, keepdims=True)
        s = scores(k_ref, page) + r[:, None, :]
        parts.append((jnp.where(own, s, NEG_INF), v_ref, page))
    rc_s[...] = rc
    accumulate(parts)

    @pl.when(p == pl.num_programs(1) - 1)
    def _():
        o_ref[...] = acc_s[...] / l_s[...]


def fox_sample_attention(q, k_new, v_new, fcum, cache_k, cache_v, cache_logf, page_table, bsz, t_len, pps=4):
    d = D_MODEL
    n_pool, page = cache_k.shape[:2]
    n_pages = page_table.shape[1]
    pps = min(pps, n_pages)
    assert n_pages % pps == 0
    ncol = t_len * N_HEADS
    hd4 = (bsz, t_len, N_HEADS, HEAD_DIM)
    qt = jnp.transpose(q.reshape(hd4), (0, 3, 1, 2)).reshape(bsz, HEAD_DIM, ncol).astype(BF16)
    kn4, vn4 = k_new.reshape(hd4), v_new.reshape(hd4)
    fc3 = fcum.reshape(bsz, t_len, 1, 1, N_HEADS)
    fnb = jnp.broadcast_to(fc3, (bsz, t_len, N_HEADS, t_len, N_HEADS)).reshape(bsz, t_len, N_HEADS, ncol)
    per_b = lambda shape: pl.BlockSpec((None,) + shape, lambda b, p, pt: (b,) + (0,) * len(shape))

    def paged(shape, i):
        return pl.BlockSpec((None,) + shape,
                            lambda b, p, pt: (pt[b, n_pages - 1 - (p * pps + i)],) + (0,) * len(shape))

    page_specs, page_ops = [], []
    for i in range(pps):
        page_specs += [paged((page, N_HEADS, HEAD_DIM), i), paged((page, N_HEADS, HEAD_DIM), i),
                       paged((page, N_HEADS), i)]
        page_ops += [cache_k, cache_v, cache_logf]
    out = pl.pallas_call(
        functools.partial(_fox_sample_kernel, t_len=t_len, page=page, pps=pps),
        out_shape=jax.ShapeDtypeStruct((bsz, HEAD_DIM, ncol), F32),
        grid_spec=pltpu.PrefetchScalarGridSpec(
            num_scalar_prefetch=1,
            grid=(bsz, n_pages // pps),
            in_specs=[per_b((HEAD_DIM, ncol)), per_b((t_len, N_HEADS, HEAD_DIM)),
                      per_b((t_len, N_HEADS, HEAD_DIM)), per_b((t_len, N_HEADS, ncol))] + page_specs,
            out_specs=per_b((HEAD_DIM, ncol)),
            scratch_shapes=[pltpu.VMEM((1, ncol), F32), pltpu.VMEM((1, ncol), F32),
                            pltpu.VMEM((HEAD_DIM, ncol), F32), pltpu.VMEM((1, ncol), F32)]),
        compiler_params=_cparams("parallel", "arbitrary"),
        name="fox_sample",
    )(page_table, qt, kn4, vn4, fnb, *page_ops)
    return jnp.transpose(out.reshape(bsz, HEAD_DIM, t_len, N_HEADS), (0, 2, 3, 1)).reshape(bsz * t_len, d)


def _trunk(x3, mods, kv_mods, shift0, wkv0, attend, p, moe_w):
    bsz, t_len, d = x3.shape
    rows = _Rows(bsz, t_len, 256)
    rows_moe = _Rows(bsz, t_len, 512)
    x = x3.reshape(bsz * t_len, d)
    shifts, states = [], []
    v_first = None
    k_sh = v_sh = lf_sh = fcum = None
    for i in range(DEPTH):
        sh_m, sc_m, gt_m, sh_f, sc_f, gt_f = jnp.split(mods[i], 6, axis=-1)
        if i < N_A:
            vmix = None if i == 0 else (v_first, p['a_v0'][i - 1], p['a_v1'][i - 1], p['a_v2'][i - 1])
            r, w, k, v, a, g, last = rwkv_proj(
                rows, x, sh_m, sc_m, shift0[i], p['norm_g'][i, 0], p['a_mu'][i], p['a_w0'][i], p['a_a0'][i],
                p['a_wr'][i], p['a_wk'][i], p['a_wv'][i], p['a_w1'][i], p['a_w2'][i], p['a_a1'][i],
                p['a_a2'][i], p['a_g1'][i], p['a_g2'][i], vmix)
            if i == 0:
                v_first = v
            shifts.append(last)
            lanes = [_to_lanes(z, bsz, t_len) for z in (r, w, k, v, a)]
            pars = [_param_lanes(z, bsz) for z in (p['a_kk'][i], p['a_ka'][i], p['a_rk'][i].reshape(-1),
                                                   p['a_lnx_w'][i], p['a_lnx_b'][i])]
            s0 = jnp.transpose(wkv0[i], (3, 2, 0, 1)).reshape(HEAD_DIM, HEAD_DIM, bsz * N_HEADS)
            y, s_t = wkv_scan(*lanes, *pars, s0)
            states.append(jnp.transpose(s_t.reshape(HEAD_DIM, HEAD_DIM, bsz, N_HEADS), (2, 3, 1, 0)))
            x = resid_mm(rows, _from_lanes(y, bsz, t_len), g, p['a_wo'][i], x, gt_m)
        else:
            j = i - N_A
            if j == 0:
                k_sh, v_sh, lf_sh, fcum = kv_proj(rows, x, kv_mods[0], kv_mods[1], p['kv_norm_g'], p['kv_w'],
                                                  p['kv_b_f'], p['k_norm_g'])
            q = q_proj(rows, x, sh_m, sc_m, p['norm_g'][i, 0], p['b_wq'][j], p['b_q_norm_g'][j])
            o = attend(q, k_sh, v_sh, fcum)
            x = resid_mm(rows, o, None, p['b_wo'][j], x, gt_m)
        if rows_moe.rows >= SPARSE_MOE_MIN_ROWS:
            xg, grp = moe_route(rows_moe, x, sh_f, sc_f, p['norm_g'][i, 1], p['router_w'], p['router_bias'])
            y = moe_sparse(xg, grp.reshape(-1), moe_w[0][i], moe_w[1][i], moe_w[2][i])
            x = resid_add(rows_moe, x, y, gt_f)
        else:
            x = moe_layer(rows_moe, x, sh_f, sc_f, gt_f, p['norm_g'][i, 1], p['router_w'], p['router_bias'],
                          moe_w[0][i], moe_w[1][i], moe_w[2][i])
    hd = (bsz, t_len, N_HEADS, HEAD_DIM)
    return (x.reshape(bsz, t_len, d), jnp.stack(shifts), jnp.stack(states), k_sh.reshape(hd),
            v_sh.reshape(hd), lf_sh.reshape(bsz, t_len, N_HEADS))


def kernel(x_prompt, x_sample, c_prompt, c_sample, state_shift, state_wkv, cache_k, cache_v, cache_logf, page_table, mod_w, mod_b, norm_g, a_mu, a_w0, a_w1, a_w2, a_a0, a_a1, a_a2, a_v0, a_v1, a_v2, a_g1, a_g2, a_kk, a_ka, a_rk, a_wr, a_wk, a_wv, a_wo, a_lnx_w, a_lnx_b, kv_mod_w, kv_mod_b, kv_norm_g, kv_w, kv_b_f, k_norm_g, b_wq, b_q_norm_g, b_wo, router_w, router_bias, moe_w_gate, moe_w_up, moe_w_down):
    p = dict(norm_g=norm_g, a_mu=a_mu, a_w0=a_w0, a_w1=a_w1, a_w2=a_w2, a_a0=a_a0, a_a1=a_a1, a_a2=a_a2,
             a_v0=a_v0, a_v1=a_v1, a_v2=a_v2, a_g1=a_g1, a_g2=a_g2, a_kk=a_kk, a_ka=a_ka, a_rk=a_rk,
             a_wr=a_wr, a_wk=a_wk, a_wv=a_wv, a_wo=a_wo, a_lnx_w=a_lnx_w, a_lnx_b=a_lnx_b,
             kv_norm_g=kv_norm_g, kv_w=kv_w, kv_b_f=kv_b_f, k_norm_g=k_norm_g, b_wq=b_wq,
             b_q_norm_g=b_q_norm_g, b_wo=b_wo, router_w=router_w, router_bias=router_bias)
    d = D_MODEL
    bp, t_p = x_prompt.shape[:2]
    bs, t_s = x_sample.shape[:2]
    moe_w = (moe_w_gate.astype(BF16), moe_w_up.astype(BF16), moe_w_down.astype(BF16))

    w_all = jnp.concatenate([mod_w[i] for i in range(DEPTH)] + [kv_mod_w], axis=1)
    b_all = jnp.concatenate([mod_b[i] for i in range(DEPTH)] + [kv_mod_b], axis=0)
    m_all = adaln_all(jnp.concatenate([c_prompt, c_sample], axis=0), w_all, b_all)

    def split_mods(m):
        layer = [m[:, i * 6 * d:(i + 1) * 6 * d] for i in range(DEPTH)]
        kvm = m[:, DEPTH * 6 * d:]
        return layer, (kvm[:, :d], kvm[:, d:])

    mods_p, kvm_p = split_mods(m_all[:bp])
    mods_s, kvm_s = split_mods(m_all[bp:])

    def attend_prompt(q, k, v, fcum):
        return fox_prompt_attention(q, k, v, fcum, bp, t_p)

    def attend_sample(q, k, v, fcum):
        return fox_sample_attention(q, k, v, fcum, cache_k, cache_v, cache_logf, page_table, bs, t_s)

    shift0_p = jnp.zeros((N_A, bp, d), F32)
    wkv0_p = jnp.zeros((N_A, bp, N_HEADS, HEAD_DIM, HEAD_DIM), F32)
    y_p, shift_p, wkv_p, k_p, v_p, lf_p = _trunk(x_prompt, mods_p, kvm_p, shift0_p, wkv0_p, attend_prompt,
                                                 p, moe_w)
    y_s, shift_s, wkv_s, k_s, v_s, lf_s = _trunk(x_sample, mods_s, kvm_s, state_shift, state_wkv,
                                                 attend_sample, p, moe_w)
    return (y_p, y_s, shift_p, wkv_p, k_p, v_p, lf_p, shift_s, wkv_s, k_s, v_s, lf_s)
```

```python
import functools

import jax
import jax.numpy as jnp
from jax import lax
from jax.experimental import pallas as pl
from jax.experimental.pallas import tpu as pltpu

D_MODEL = 1024
HEAD_DIM = 64
N_HEADS = D_MODEL // HEAD_DIM
N_EXPERTS = 16
N_GROUPS = 4
EXPERTS_PER_GROUP = N_EXPERTS // N_GROUPS
D_EXPERT = 512
DEPTH = 4
N_A = 2
RMS_EPS = 1e-6
GN_EPS = 64e-5
LANES = 128
VMEM_LIMIT = 56 * 1024 * 1024
SPARSE_MOE_MIN_ROWS = 4096

F32 = jnp.float32
BF16 = jnp.bfloat16
NEG_INF = float("-inf")


def _cparams(*sem):
    return pltpu.CompilerParams(dimension_semantics=sem, vmem_limit_bytes=VMEM_LIMIT)


def _bdot(a, b):
    return jnp.dot(a.astype(BF16), b.astype(BF16), preferred_element_type=F32)


def _dot3(a, b):
    a = a.astype(F32)
    b = b.astype(F32)
    a1 = a.astype(BF16)
    a2 = (a - a1.astype(F32)).astype(BF16)
    b1 = b.astype(BF16)
    b2 = (b - b1.astype(F32)).astype(BF16)
    return (jnp.dot(a1, b1, preferred_element_type=F32)
            + (jnp.dot(a1, b2, preferred_element_type=F32) + jnp.dot(a2, b1, preferred_element_type=F32)))


def _mm(a, b, precise):
    return _dot3(a, b) if precise else _bdot(a, b)


def _split_dot(a, m):
    a1 = a.astype(BF16)
    r1 = a - a1.astype(F32)
    a2 = r1.astype(BF16)
    a3 = (r1 - a2.astype(F32)).astype(BF16)
    return (jnp.dot(a1, m, preferred_element_type=F32) + jnp.dot(a2, m, preferred_element_type=F32)
            + jnp.dot(a3, m, preferred_element_type=F32))


def _sigmoid(z):
    return 1.0 / (1.0 + jnp.exp(-z))


def _silu(z):
    return z * _sigmoid(z)


def _softplus(z):
    return jnp.maximum(z, 0.0) + jnp.log(1.0 + jnp.exp(-jnp.abs(z)))


def _modulate(x, g, sh, sc):
    y = x * lax.rsqrt(jnp.mean(x * x, axis=-1, keepdims=True) + RMS_EPS)
    return (y * g) * (1.0 + sc) + sh


def _head_rms(z, gsum, gexp, g_row):
    ss = _split_dot(z * z, gsum)
    inv = lax.rsqrt(ss * (1.0 / HEAD_DIM) + RMS_EPS)
    return z * _split_dot(inv, gexp) * g_row


def _group_mats():
    head_of_lane = jnp.arange(D_MODEL) // HEAD_DIM
    gsum = (head_of_lane[:, None] == jnp.arange(LANES)[None, :]).astype(BF16)
    return gsum, gsum.T


class _Rows:
    def __init__(self, bsz, t_len, tile, precise=False):
        self.bsz, self.t_len = bsz, t_len
        self.precise = precise
        self.rows = bsz * t_len
        self.tm = min(tile, self.rows)
        self.long = t_len >= self.tm
        if self.long:
            assert t_len % self.tm == 0
        else:
            assert self.tm % t_len == 0
        assert self.rows % self.tm == 0
        self.n_tiles = self.rows // self.tm
        self.tiles_per_seq = t_len // self.tm if self.long else 1

    def row_spec(self, width=D_MODEL):
        return pl.BlockSpec((self.tm, width), lambda i, *_: (i, 0))

    def mod_operand(self, m):
        w = m.shape[-1]
        if self.long:
            tps = self.tiles_per_seq
            return m.reshape(self.bsz, 1, w), pl.BlockSpec((None, 1, w), lambda i, *_: (i // tps, 0, 0))
        return jnp.repeat(m, self.t_len, axis=0), pl.BlockSpec((self.tm, w), lambda i, *_: (i, 0))


def _full_spec(a):
    nd = a.ndim
    return pl.BlockSpec(a.shape, lambda *_: (0,) * nd)


def _adaln_kernel(c_ref, w_ref, b_ref, o_ref):
    c = c_ref[...]
    o_ref[...] = jnp.dot(_silu(c), w_ref[...], precision=lax.Precision.HIGHEST,
                         preferred_element_type=F32) + b_ref[...]


def adaln_all(c, w, b, tn=1024):
    m, k = c.shape
    n = w.shape[1]
    return pl.pallas_call(
        _adaln_kernel,
        out_shape=jax.ShapeDtypeStruct((m, n), F32),
        grid=(n // tn,),
        in_specs=[pl.BlockSpec((m, k), lambda j: (0, 0)), pl.BlockSpec((k, tn), lambda j: (0, j)),
                  pl.BlockSpec((1, tn), lambda j: (0, j))],
        out_specs=pl.BlockSpec((m, tn), lambda j: (0, j)),
        compiler_params=_cparams("parallel"),
        name="adaln",
    )(c, w, b.reshape(1, n))


def _rwkv_proj_kernel(*refs, rows, has_vmix):
    (x_ref, sh_ref, sc_ref, first_ref, g_ref, mu_ref, w0_ref, a0_ref,
     wr_ref, wk_ref, wv_ref, w1_ref, w2_ref, a1_ref, a2_ref, g1_ref, g2_ref) = refs[:17]
    pos = 17
    if has_vmix:
        vf_ref, v0_ref, v1_ref, v2_ref = refs[pos:pos + 4]
        pos += 4
    r_ref, w_ref, k_ref, v_ref, a_ref, gg_ref, last_ref = refs[pos:pos + 7]
    carry = refs[pos + 7]
    tm = rows.tm
    i = pl.program_id(0)

    xn = _modulate(x_ref[...], g_ref[...], sh_ref[...], sc_ref[...])
    rolled = pltpu.roll(xn, 1, 0)
    row = lax.broadcasted_iota(jnp.int32, (tm, 1), 0)
    if rows.long:
        @pl.when(i % rows.tiles_per_seq == 0)
        def _():
            carry[...] = first_ref[...]
        x_prev = jnp.where(row == 0, carry[...], rolled)
        carry[...] = xn[tm - 1:tm, :]
        last_ref[...] = xn[tm - 1:tm, :]
    else:
        x_prev = jnp.where(row % rows.t_len == 0, first_ref[...], rolled)
        last_ref[...] = xn
    xx = x_prev - xn
    mu = mu_ref[...]

    mm = functools.partial(_mm, precise=rows.precise)

    def mix(j):
        z = xn + xx * mu[j:j + 1, :]
        return z if rows.precise else z.astype(BF16)

    xv = mix(3)
    r_ref[...] = mm(mix(0), wr_ref[...])
    w_ref[...] = -_softplus(-(w0_ref[...] + mm(jnp.tanh(mm(mix(1), w1_ref[...])), w2_ref[...]))) - 0.5
    k_ref[...] = mm(mix(2), wk_ref[...])
    v = mm(xv, wv_ref[...])
    if has_vmix:
        v = v + (vf_ref[...] - v) * _sigmoid(v0_ref[...] + mm(mm(xv, v1_ref[...]), v2_ref[...]))
    v_ref[...] = v
    a_ref[...] = _sigmoid(a0_ref[...] + mm(mm(mix(4), a1_ref[...]), a2_ref[...]))
    gg_ref[...] = mm(_sigmoid(mm(mix(5), g1_ref[...])), g2_ref[...])


def rwkv_proj(rows, x, sh, sc, shift0, norm_g, mu, w0, a0, wr, wk, wv, w1, w2, a1, a2, g1, g2, vmix):
    d = D_MODEL
    sh_op, sh_spec = rows.mod_operand(sh)
    sc_op, sc_spec = rows.mod_operand(sc)
    first_op, first_spec = rows.mod_operand(shift0)
    row2 = lambda z: z.reshape(1, -1)
    wcast = (lambda z: z) if rows.precise else (lambda z: z.astype(BF16))
    ops = [x, sh_op, sc_op, first_op, row2(norm_g), mu, row2(w0), row2(a0)] + [
        wcast(z) for z in (wr, wk, wv, w1, w2, a1, a2, g1, g2)]
    specs = [rows.row_spec(), sh_spec, sc_spec, first_spec] + [_full_spec(o) for o in ops[4:]]
    if vmix is not None:
        v_first, v0, v1, v2 = vmix
        extra = [v_first, row2(v0), wcast(v1), wcast(v2)]
        ops += extra
        specs += [rows.row_spec()] + [_full_spec(o) for o in extra[1:]]
    act = jax.ShapeDtypeStruct((rows.rows, d), F32)
    if rows.long:
        tps = rows.tiles_per_seq
        last_shape = jax.ShapeDtypeStruct((rows.bsz, 1, d), F32)
        last_spec = pl.BlockSpec((None, 1, d), lambda i: (i // tps, 0, 0))
    else:
        last_shape, last_spec = act, rows.row_spec()
    outs = pl.pallas_call(
        functools.partial(_rwkv_proj_kernel, rows=rows, has_vmix=vmix is not None),
        out_shape=(act,) * 6 + (last_shape,),
        grid=(rows.n_tiles,),
        in_specs=specs,
        out_specs=(rows.row_spec(),) * 6 + (last_spec,),
        scratch_shapes=[pltpu.VMEM((1, d), F32)],
        compiler_params=_cparams("arbitrary"),
        name="rwkv_proj",
    )(*ops)
    last = outs[6]
    last = last.reshape(rows.bsz, d) if rows.long else last.reshape(rows.bsz, rows.t_len, d)[:, -1]
    return outs[:6] + (last,)


def _wkv_kernel(r_ref, w_ref, k_ref, v_ref, a_ref, kk_ref, ka_ref, rk_ref, lw_ref, lb_ref, s0_ref,
                y_ref, st_ref, S, nk_s, bb_s, km_s, dc_s, y_s, *, tt):
    tb = pl.program_id(1)

    @pl.when(tb == 0)
    def _():
        S[...] = s0_ref[...]

    k = k_ref[...]
    a = a_ref[...]
    kkr = k * kk_ref[...][None]
    ss = jnp.sum(kkr * kkr, axis=1, keepdims=True)
    kk = kkr * lax.rsqrt(jnp.maximum(ss, 1e-24))
    km = k * (1.0 + (a - 1.0) * ka_ref[...][None])
    nk_s[...] = -kk
    bb_s[...] = kk * a
    km_s[...] = km
    dc_s[...] = jnp.exp(-jnp.exp(w_ref[...]))

    def step(t, carry):
        def row(ref, j):
            return jnp.broadcast_to(ref[t, pl.ds(j, 1), :], (HEAD_DIM, LANES))

        sa = S[0] * row(nk_s, 0)
        for j in range(1, HEAD_DIM):
            sa = sa + S[j] * row(nk_s, j)
        vt = v_ref[t]
        y = None
        for j in range(HEAD_DIM):
            s_new = S[j] * row(dc_s, j) + sa * row(bb_s, j) + vt * row(km_s, j)
            S[j] = s_new
            yj = s_new * row(r_ref, j)
            y = yj if y is None else y + yj
        y_s[t] = y
        return carry

    lax.fori_loop(0, tt, step, 0)

    y = y_s[...]
    mean = jnp.mean(y, axis=1, keepdims=True)
    yc = y - mean
    var = jnp.mean(yc * yc, axis=1, keepdims=True)
    bonus = jnp.sum(r_ref[...] * km * rk_ref[...][None], axis=1, keepdims=True) * v_ref[...]
    y_ref[...] = yc * lax.rsqrt(var + GN_EPS) * lw_ref[...][None] + lb_ref[...][None] + bonus

    @pl.when(tb == pl.num_programs(1) - 1)
    def _():
        st_ref[...] = S[...]


def wkv_scan(r, w, k, v, a, kk_p, ka_p, rk_p, lw_p, lb_p, s0, *, tt=16):
    t_len, n, l_tot = r.shape
    tt = min(tt, t_len)
    assert t_len % tt == 0 and l_tot % LANES == 0
    seq_spec = pl.BlockSpec((tt, n, LANES), lambda g, t: (t, 0, g))
    par_spec = pl.BlockSpec((n, LANES), lambda g, t: (0, g))
    st_spec = pl.BlockSpec((n, n, LANES), lambda g, t: (0, 0, g))
    return pl.pallas_call(
        functools.partial(_wkv_kernel, tt=tt),
        out_shape=(jax.ShapeDtypeStruct(r.shape, F32), jax.ShapeDtypeStruct(s0.shape, F32)),
        grid=(l_tot // LANES, t_len // tt),
        in_specs=[seq_spec] * 5 + [par_spec] * 5 + [st_spec],
        out_specs=(seq_spec, st_spec),
        scratch_shapes=[pltpu.VMEM((n, n, LANES), F32)] + [pltpu.VMEM((tt, n, LANES), F32)] * 5,
        compiler_params=_cparams("parallel", "arbitrary"),
        name="wkv_scan",
    )(r, w, k, v, a, kk_p, ka_p, rk_p, lw_p, lb_p, s0)


def _to_lanes(z, bsz, t_len):
    return jnp.transpose(z.reshape(bsz, t_len, N_HEADS, HEAD_DIM), (1, 3, 0, 2)).reshape(
        t_len, HEAD_DIM, bsz * N_HEADS)


def _from_lanes(z, bsz, t_len):
    return jnp.transpose(z.reshape(t_len, HEAD_DIM, bsz, N_HEADS), (2, 0, 3, 1)).reshape(
        bsz * t_len, D_MODEL)


def _param_lanes(p, bsz):
    return jnp.broadcast_to(p.reshape(1, N_HEADS, HEAD_DIM), (bsz, N_HEADS, HEAD_DIM)).transpose(
        2, 0, 1).reshape(HEAD_DIM, bsz * N_HEADS)


def _resid_mm_kernel(*refs, has_mul, precise):
    if has_mul:
        a_ref, b_ref, w_ref, x_ref, gt_ref, o_ref = refs
        lhs = a_ref[...] * b_ref[...]
    else:
        a_ref, w_ref, x_ref, gt_ref, o_ref = refs
        lhs = a_ref[...]
    o_ref[...] = x_ref[...] + gt_ref[...] * _mm(lhs, w_ref[...], precise)


def resid_mm(rows, a, b, w, x, gate):
    gt_op, gt_spec = rows.mod_operand(gate)
    w = w if rows.precise else w.astype(BF16)
    ops = [a] + ([b] if b is not None else []) + [w, x, gt_op]
    specs = [rows.row_spec()] * (2 if b is not None else 1) + [_full_spec(w), rows.row_spec(), gt_spec]
    return pl.pallas_call(
        functools.partial(_resid_mm_kernel, has_mul=b is not None, precise=rows.precise),
        out_shape=jax.ShapeDtypeStruct(x.shape, F32),
        grid=(rows.n_tiles,),
        in_specs=specs,
        out_specs=rows.row_spec(),
        compiler_params=_cparams("parallel"),
        name="resid_mm",
    )(*ops)


def _route(probs, bias):
    tm = probs.shape[0]
    sel = probs + bias
    lane = lax.broadcasted_iota(jnp.int32, (tm, N_EXPERTS), 1)
    grp = lane // EXPERTS_PER_GROUP

    def top2(mask):
        x = jnp.where(mask, sel, NEG_INF)
        v1 = jnp.max(x, axis=-1, keepdims=True)
        i1 = jnp.min(jnp.where(x == v1, lane, N_EXPERTS), axis=-1, keepdims=True)
        x2 = jnp.where(lane == i1, NEG_INF, x)
        v2 = jnp.max(x2, axis=-1, keepdims=True)
        i2 = jnp.min(jnp.where(x2 == v2, lane, N_EXPERTS), axis=-1, keepdims=True)
        return v1 + v2, i1, i2

    best_score, _, _ = top2(grp == 0)
    best = jnp.zeros((tm, 1), jnp.int32)
    for g in range(1, N_GROUPS):
        score, _, _ = top2(grp == g)
        better = score > best_score
        best = jnp.where(better, g, best)
        best_score = jnp.where(better, score, best_score)
    _, i1, i2 = top2(grp == best)
    chosen = (lane == i1) | (lane == i2)
    w = jnp.where(chosen, probs, 0.0)
    return w / jnp.sum(w, axis=-1, keepdims=True), best


def _router_gate(xn, rw, rb):
    logits = jnp.dot(xn, rw, precision=lax.Precision.HIGHEST, preferred_element_type=F32)
    z = jnp.exp(logits - jnp.max(logits, axis=-1, keepdims=True))
    return _route(z / jnp.sum(z, axis=-1, keepdims=True), rb)


def _moe_kernel(x_ref, sh_ref, sc_ref, gt_ref, g_ref, rw_ref, rb_ref, wg_ref, wu_ref, wd_ref,
                o_ref, xn_s, gate_s, acc_s, *, precise):
    e = pl.program_id(1)

    @pl.when(e == 0)
    def _():
        xn = _modulate(x_ref[...], g_ref[...], sh_ref[...], sc_ref[...])
        xn_s[...] = xn.astype(xn_s.dtype)
        gate_s[...], _ = _router_gate(xn, rw_ref[...], rb_ref[...])
        acc_s[...] = jnp.zeros_like(acc_s)

    acc_s[...] += _expert_ffn(xn_s[...], gate_s[...], e, wg_ref, wu_ref, wd_ref, precise)

    @pl.when(e == pl.num_programs(1) - 1)
    def _():
        o_ref[...] = x_ref[...] + gt_ref[...] * acc_s[...]


def _expert_ffn(xn, gate, expert, wg_ref, wu_ref, wd_ref, precise=False):
    lane = lax.broadcasted_iota(jnp.int32, gate.shape, 1)
    g_e = jnp.sum(jnp.where(lane == expert, gate, 0.0), axis=-1, keepdims=True)
    h = _silu(_mm(xn, wg_ref[...], precise)) * _mm(xn, wu_ref[...], precise)
    return _mm(h * g_e, wd_ref[...], precise)


def moe_layer(rows, x, sh, sc, gate, norm_g, router_w, router_bias, w_gate, w_up, w_down):
    d = D_MODEL
    sh_op, sh_spec = rows.mod_operand(sh)
    sc_op, sc_spec = rows.mod_operand(sc)
    gt_op, gt_spec = rows.mod_operand(gate)
    g2 = norm_g.reshape(1, d)
    rb = router_bias.reshape(1, N_EXPERTS)
    tm = rows.tm
    return pl.pallas_call(
        functools.partial(_moe_kernel, precise=rows.precise),
        out_shape=jax.ShapeDtypeStruct(x.shape, F32),
        grid=(rows.n_tiles, N_EXPERTS),
        in_specs=[rows.row_spec(), sh_spec, sc_spec, gt_spec, _full_spec(g2), _full_spec(router_w),
                  _full_spec(rb),
                  pl.BlockSpec((None, d, D_EXPERT), lambda i, e: (e, 0, 0)),
                  pl.BlockSpec((None, d, D_EXPERT), lambda i, e: (e, 0, 0)),
                  pl.BlockSpec((None, D_EXPERT, d), lambda i, e: (e, 0, 0))],
        out_specs=rows.row_spec(),
        scratch_shapes=[pltpu.VMEM((tm, d), F32 if rows.precise else BF16),
                        pltpu.VMEM((tm, N_EXPERTS), F32), pltpu.VMEM((tm, d), F32)],
        compiler_params=_cparams("parallel", "arbitrary"),
        name="moe",
    )(x, sh_op, sc_op, gt_op, g2, router_w, rb, w_gate, w_up, w_down)


PAYLOAD = D_MODEL + LANES


def _moe_route_kernel(x_ref, sh_ref, sc_ref, g_ref, rw_ref, rb_ref, xg_ref, grp_ref):
    xn = _modulate(x_ref[...], g_ref[...], sh_ref[...], sc_ref[...])
    gate, best = _router_gate(xn, rw_ref[...], rb_ref[...])
    xg_ref[:, :D_MODEL] = xn
    xg_ref[:, D_MODEL:] = jnp.zeros((xn.shape[0], LANES), F32)
    xg_ref[:, D_MODEL:D_MODEL + N_EXPERTS] = gate
    grp_ref[...] = best


def moe_route(rows, x, sh, sc, norm_g, router_w, router_bias):
    d = D_MODEL
    sh_op, sh_spec = rows.mod_operand(sh)
    sc_op, sc_spec = rows.mod_operand(sc)
    g2 = norm_g.reshape(1, d)
    rb = router_bias.reshape(1, N_EXPERTS)
    return pl.pallas_call(
        _moe_route_kernel,
        out_shape=(jax.ShapeDtypeStruct((rows.rows, PAYLOAD), F32),
                   jax.ShapeDtypeStruct((rows.rows, 1), jnp.int32)),
        grid=(rows.n_tiles,),
        in_specs=[rows.row_spec(), sh_spec, sc_spec, _full_spec(g2), _full_spec(router_w), _full_spec(rb)],
        out_specs=(rows.row_spec(PAYLOAD), rows.row_spec(1)),
        compiler_params=_cparams("parallel"),
        name="moe_route",
    )(x, sh_op, sc_op, g2, router_w, rb)


def _group_order(grp, tm):
    n = grp.shape[0]
    p_slots = n + N_GROUPS * tm
    onehot = (grp[:, None] == jnp.arange(N_GROUPS, dtype=jnp.int32)[None, :]).astype(jnp.int32)
    counts = jnp.sum(onehot, axis=0)
    rank = jnp.sum((jnp.cumsum(onehot, axis=0) - onehot) * onehot, axis=1)
    padded = ((counts + tm - 1) // tm) * tm
    ends = jnp.cumsum(padded)
    pos = jnp.sum(onehot * (ends - padded)[None, :], axis=1) + rank
    token = jnp.arange(n, dtype=jnp.int32)
    filled = jnp.zeros((p_slots + 2 * tm,), jnp.int32).at[pos].set(token + 1)
    src = jnp.maximum(filled - 1, 0)
    real = jnp.minimum(filled[:p_slots], 1)
    dst = jnp.where(real == 1, src[:p_slots], n + jnp.cumsum(1 - real) - 1).astype(jnp.int32)
    dst_prev = jnp.concatenate([p_slots + jnp.arange(tm, dtype=jnp.int32), dst])
    tile_start = jnp.arange(p_slots // tm + 1, dtype=jnp.int32) * tm
    tile_grp = jnp.sum((tile_start[:, None] >= ends[None, :]).astype(jnp.int32), axis=1)
    tile_grp = jnp.where(tile_grp < N_GROUPS, tile_grp, -1).astype(jnp.int32)
    return src, dst_prev, tile_grp


def _moe_sparse_kernel(src_ref, dst_ref, tg_ref, xg_hbm, wg_ref, wu_ref, wd_ref, y_hbm,
                       xbuf, ybuf, xn_s, gate_s, acc_s, gsem, ssem, *, tm):
    i = pl.program_id(0)
    e = pl.program_id(1)
    nt = pl.num_programs(0)
    slot = i % 2
    prev = 1 - slot
    quarter = tm // EXPERTS_PER_GROUP

    def row_in(tile, to_slot, k):
        return pltpu.make_async_copy(xg_hbm.at[pl.ds(src_ref[tile * tm + k], 1), :],
                                     xbuf.at[to_slot, pl.ds(k, 1), :], gsem.at[to_slot])

    def row_out(tile, from_slot, k):
        return pltpu.make_async_copy(ybuf.at[from_slot, pl.ds(k, 1), :],
                                     y_hbm.at[pl.ds(dst_ref[tile * tm + k], 1), :], ssem.at[from_slot])

    def wait_in(slot_):
        pltpu.make_async_copy(xg_hbm.at[pl.ds(0, tm), :], xbuf.at[slot_], gsem.at[slot_]).wait()

    def wait_out(slot_):
        pltpu.make_async_copy(ybuf.at[slot_], y_hbm.at[pl.ds(0, tm), :], ssem.at[slot_]).wait()

    @pl.when((i == 0) & (e == 0))
    def _():
        def body(k, c):
            row_in(0, 0, k).start()
            return c
        lax.fori_loop(0, tm, body, 0)
        ybuf[1] = jnp.zeros((tm, D_MODEL), F32)

    @pl.when(e == 0)
    def _():
        wait_in(slot)
        xn_s[...] = xbuf[slot, :, :D_MODEL].astype(BF16)
        gate_s[...] = xbuf[slot, :, D_MODEL:]

    def start_neighbour_rows():
        for kk in range(quarter):
            k = e * quarter + kk
            row_in(i + 1, prev, k).start()
            row_out(i, prev, k).start()

    grp = tg_ref[i]

    @pl.when(grp >= 0)
    def _():
        start_neighbour_rows()
        contrib = _expert_ffn(xn_s[...], gate_s[...], grp * EXPERTS_PER_GROUP + e, wg_ref, wu_ref, wd_ref)

        @pl.when(e == 0)
        def _():
            acc_s[...] = contrib

        @pl.when(e > 0)
        def _():
            acc_s[...] += contrib

    @pl.when(grp < 0)
    def _():
        start_neighbour_rows()

    @pl.when(e == EXPERTS_PER_GROUP - 1)
    def _():
        @pl.when(i >= 1)
        def _():
            wait_out(slot)

        @pl.when(grp >= 0)
        def _():
            ybuf[slot] = acc_s[...]

        @pl.when(i == nt - 1)
        def _():
            wait_out(prev)
            wait_in(prev)


def moe_sparse(xg, grp, w_gate, w_up, w_down, tm=512):
    d = D_MODEL
    src, dst_prev, tile_grp = _group_order(grp, tm)
    n_tiles = tile_grp.shape[0]
    expert_of = lambda i, e, s, t, tg: (jnp.maximum(tg[i], 0) * EXPERTS_PER_GROUP + e, 0, 0)
    return pl.pallas_call(
        functools.partial(_moe_sparse_kernel, tm=tm),
        out_shape=jax.ShapeDtypeStruct((n_tiles * tm, d), F32),
        grid_spec=pltpu.PrefetchScalarGridSpec(
            num_scalar_prefetch=3,
            grid=(n_tiles, EXPERTS_PER_GROUP),
            in_specs=[pl.BlockSpec(memory_space=pl.ANY),
                      pl.BlockSpec((None, d, D_EXPERT), expert_of),
                      pl.BlockSpec((None, d, D_EXPERT), expert_of),
                      pl.BlockSpec((None, D_EXPERT, d), expert_of)],
            out_specs=pl.BlockSpec(memory_space=pl.ANY),
            scratch_shapes=[pltpu.VMEM((2, tm, PAYLOAD), F32), pltpu.VMEM((2, tm, d), F32),
                            pltpu.VMEM((tm, d), BF16), pltpu.VMEM((tm, LANES), F32),
                            pltpu.VMEM((tm, d), F32),
                            pltpu.SemaphoreType.DMA((2,)), pltpu.SemaphoreType.DMA((2,))]),
        compiler_params=_cparams("arbitrary", "arbitrary"),
        name="moe_sparse",
    )(src, dst_prev, tile_grp, xg, w_gate, w_up, w_down)


def _resid_add_kernel(x_ref, y_ref, gt_ref, o_ref):
    o_ref[...] = x_ref[...] + gt_ref[...] * y_ref[...]


def resid_add(rows, x, y, gate):
    gt_op, gt_spec = rows.mod_operand(gate)
    return pl.pallas_call(
        _resid_add_kernel,
        out_shape=jax.ShapeDtypeStruct(x.shape, F32),
        grid=(rows.n_tiles,),
        in_specs=[rows.row_spec(), rows.row_spec(), gt_spec],
        out_specs=rows.row_spec(),
        compiler_params=_cparams("parallel"),
        name="resid_add",
    )(x, y, gt_op)


def _kv_proj_kernel(x_ref, sh_ref, sc_ref, g_ref, wk_ref, wv_ref, wf_ref, bf_ref, kg_ref, gsum_ref,
                    gexp_ref, k_ref, v_ref, lf_ref, fc_ref, carry, *, rows):
    tm = rows.tm
    i = pl.program_id(0)
    xn = _modulate(x_ref[...], g_ref[...], sh_ref[...], sc_ref[...]).astype(BF16)
    k_ref[...] = _head_rms(jnp.dot(xn, wk_ref[...], preferred_element_type=F32), gsum_ref[...],
                           gexp_ref[...], kg_ref[...])
    v_ref[...] = jnp.dot(xn, wv_ref[...], preferred_element_type=F32)
    z = jnp.dot(xn, wf_ref[...], preferred_element_type=F32) + bf_ref[...]
    lf = -_softplus(-z)
    lf_ref[...] = lf[:, :N_HEADS]
    ri = lax.broadcasted_iota(jnp.int32, (tm, tm), 0)
    ci = lax.broadcasted_iota(jnp.int32, (tm, tm), 1)
    if rows.long:
        tri = (ci <= ri).astype(BF16)

        @pl.when(i % rows.tiles_per_seq == 0)
        def _():
            carry[...] = jnp.zeros_like(carry)
        cum = _split_dot_lhs_exact(tri, lf) + carry[...]
        carry[...] = cum[tm - 1:tm, :]
    else:
        tri = ((ci <= ri) & (ci // rows.t_len == ri // rows.t_len)).astype(BF16)
        cum = _split_dot_lhs_exact(tri, lf)
    fc_ref[...] = cum[:, :N_HEADS]


def _split_dot_lhs_exact(m, b):
    b1 = b.astype(BF16)
    r1 = b - b1.astype(F32)
    b2 = r1.astype(BF16)
    b3 = (r1 - b2.astype(F32)).astype(BF16)
    return (jnp.dot(m, b1, preferred_element_type=F32) + jnp.dot(m, b2, preferred_element_type=F32)
            + jnp.dot(m, b3, preferred_element_type=F32))


def kv_proj(rows, x, sh, sc, norm_g, kv_w, kv_b_f, k_norm_g):
    d = D_MODEL
    sh_op, sh_spec = rows.mod_operand(sh)
    sc_op, sc_spec = rows.mod_operand(sc)
    gsum, gexp = _group_mats()
    wk = kv_w[:, :d].astype(BF16)
    wv = kv_w[:, d:2 * d].astype(BF16)
    wf = jnp.pad(kv_w[:, 2 * d:], ((0, 0), (0, LANES - N_HEADS))).astype(BF16)
    bf = jnp.pad(kv_b_f, (0, LANES - N_HEADS)).reshape(1, LANES)
    kg = jnp.tile(k_norm_g, N_HEADS).reshape(1, d)
    consts = [norm_g.reshape(1, d), wk, wv, wf, bf, kg, gsum, gexp]
    act = jax.ShapeDtypeStruct((rows.rows, d), F32)
    small = jax.ShapeDtypeStruct((rows.rows, N_HEADS), F32)
    return pl.pallas_call(
        functools.partial(_kv_proj_kernel, rows=rows),
        out_shape=(act, act, small, small),
        grid=(rows.n_tiles,),
        in_specs=[rows.row_spec(), sh_spec, sc_spec] + [_full_spec(c) for c in consts],
        out_specs=(rows.row_spec(), rows.row_spec(), rows.row_spec(N_HEADS), rows.row_spec(N_HEADS)),
        scratch_shapes=[pltpu.VMEM((1, LANES), F32)],
        compiler_params=_cparams("arbitrary"),
        name="kv_proj",
    )(x, sh_op, sc_op, *consts)


def _q_proj_kernel(x_ref, sh_ref, sc_ref, g_ref, wq_ref, qg_ref, gsum_ref, gexp_ref, q_ref):
    xn = _modulate(x_ref[...], g_ref[...], sh_ref[...], sc_ref[...])
    q = _head_rms(_bdot(xn, wq_ref[...]), gsum_ref[...], gexp_ref[...], qg_ref[...])
    q_ref[...] = q * (HEAD_DIM ** -0.5)


def q_proj(rows, x, sh, sc, norm_g, wq, q_norm_g):
    d = D_MODEL
    sh_op, sh_spec = rows.mod_operand(sh)
    sc_op, sc_spec = rows.mod_operand(sc)
    gsum, gexp = _group_mats()
    consts = [norm_g.reshape(1, d), wq.astype(BF16), jnp.tile(q_norm_g, N_HEADS).reshape(1, d), gsum, gexp]
    return pl.pallas_call(
        _q_proj_kernel,
        out_shape=jax.ShapeDtypeStruct((rows.rows, d), F32),
        grid=(rows.n_tiles,),
        in_specs=[rows.row_spec(), sh_spec, sc_spec] + [_full_spec(c) for c in consts],
        out_specs=rows.row_spec(),
        compiler_params=_cparams("parallel"),
        name="q_proj",
    )(x, sh_op, sc_op, *consts)


def _fox_prompt_kernel(q_ref, k_ref, v_ref, f_ref, o_ref, m_s, l_s, acc_s, *, tile):
    hp = pl.program_id(1)
    qi = pl.program_id(2)
    lane = lax.broadcasted_iota(jnp.int32, (1, LANES), 1)
    q = q_ref[...]
    qms = []
    for hh in range(2):
        qms.append(jnp.where((lane // HEAD_DIM) == hh, q, 0.0).astype(BF16))
        m_s[hh] = jnp.full((1, tile), NEG_INF, F32)
        l_s[hh] = jnp.zeros((1, tile), F32)
        acc_s[hh] = jnp.zeros((LANES, tile), F32)

    def step(ki, diagonal):
        start = pl.multiple_of(ki * tile, tile)
        kt = k_ref[pl.ds(start, tile), :].astype(BF16)
        vt = v_ref[pl.ds(start, tile), :].astype(BF16)
        fk = f_ref[pl.ds(start, tile), :]
        lane_h = lax.broadcasted_iota(jnp.int32, fk.shape, 1)
        for hh in range(2):
            f_key = jnp.sum(jnp.where(lane_h == 2 * hp + hh, fk, 0.0), axis=-1, keepdims=True)
            s = lax.dot_general(kt, qms[hh], (((1,), (1,)), ((), ())), preferred_element_type=F32) - f_key
            if diagonal:
                key = lax.broadcasted_iota(jnp.int32, (tile, tile), 0)
                qry = lax.broadcasted_iota(jnp.int32, (tile, tile), 1)
                s = jnp.where(key <= qry, s, NEG_INF)
            m_old = m_s[hh]
            m_new = jnp.maximum(m_old, jnp.max(s, axis=0, keepdims=True))
            p = jnp.exp(s - m_new)
            alpha = jnp.exp(m_old - m_new)
            l_s[hh] = alpha * l_s[hh] + jnp.sum(p, axis=0, keepdims=True)
            acc_s[hh] = alpha * acc_s[hh] + lax.dot_general(
                vt, p.astype(BF16), (((0,), (0,)), ((), ())), preferred_element_type=F32)
            m_s[hh] = m_new

    def body(ki, c):
        step(ki, False)
        return c

    lax.fori_loop(0, qi, body, 0)
    step(qi, True)
    chan = lax.broadcasted_iota(jnp.int32, (LANES, 1), 0)
    o_t = jnp.where((chan // HEAD_DIM) == 0, acc_s[0] / l_s[0], acc_s[1] / l_s[1])
    o_ref[...] = o_t.T


def fox_prompt_attention(q, k, v, fcum, bsz, t_len, tile=512):
    d = D_MODEL
    tile = min(tile, t_len)
    q3, k3, v3 = (z.reshape(bsz, t_len, d) for z in (q, k, v))
    f3 = fcum.reshape(bsz, t_len, N_HEADS)
    out = pl.pallas_call(
        functools.partial(_fox_prompt_kernel, tile=tile),
        out_shape=jax.ShapeDtypeStruct((bsz, t_len, d), F32),
        grid=(bsz, N_HEADS // 2, t_len // tile),
        in_specs=[pl.BlockSpec((None, tile, LANES), lambda b, h, i: (b, i, h)),
                  pl.BlockSpec((None, t_len, LANES), lambda b, h, i: (b, 0, h)),
                  pl.BlockSpec((None, t_len, LANES), lambda b, h, i: (b, 0, h)),
                  pl.BlockSpec((None, t_len, N_HEADS), lambda b, h, i: (b, 0, 0))],
        out_specs=pl.BlockSpec((None, tile, LANES), lambda b, h, i: (b, i, h)),
        scratch_shapes=[pltpu.VMEM((2, 1, tile), F32), pltpu.VMEM((2, 1, tile), F32),
                        pltpu.VMEM((2, LANES, tile), F32)],
        compiler_params=_cparams("parallel", "parallel", "arbitrary"),
        name="fox_prompt",
    )(q3, k3, v3, f3)
    return out.reshape(bsz * t_len, d)


def _fox_sample_kernel(pt_ref, q_ref, kn_ref, vn_ref, fnr_ref, *rest, t_len, page, pps):
    page_refs = rest[:3 * pps]
    o_ref, qbd_s, m_s, l_s, acc_s, rc_s = rest[3 * pps:]
    p = pl.program_id(1)
    nrow = t_len * N_HEADS
    head_of_row = lax.broadcasted_iota(jnp.int32, (nrow, D_MODEL), 0) % N_HEADS
    head_of_col = lax.broadcasted_iota(jnp.int32, (nrow, D_MODEL), 1) // HEAD_DIM
    diag = head_of_row == head_of_col

    @pl.when(p == 0)
    def _():
        q = q_ref[...]
        qrep = jnp.broadcast_to(q[:, None, :], (t_len, N_HEADS, D_MODEL)).reshape(nrow, D_MODEL)
        qbd = jnp.where(diag, qrep, 0.0)
        qbd_s[...] = qbd.astype(BF16)
        q_of_row = lax.broadcasted_iota(jnp.int32, (nrow, 1), 0) // N_HEADS
        fnr = fnr_ref[...]
        s_cols = []
        for t in range(t_len):
            s_t = jnp.sum(qbd * kn_ref[pl.ds(t, 1), :], axis=-1, keepdims=True)
            s_cols.append(jnp.where(q_of_row >= t, s_t - fnr[:, t:t + 1], NEG_INF))
        m = functools.reduce(jnp.maximum, s_cols)
        l = jnp.zeros((nrow, 1), F32)
        acc = jnp.zeros((nrow, D_MODEL), F32)
        for t in range(t_len):
            p_t = jnp.exp(s_cols[t] - m)
            l = l + p_t
            acc = acc + p_t * vn_ref[pl.ds(t, 1), :]
        m_s[...] = m
        l_s[...] = l
        acc_s[...] = acc
        rc_s[...] = jnp.zeros_like(rc_s)

    ji = lax.broadcasted_iota(jnp.int32, (page, page), 0)
    si = lax.broadcasted_iota(jnp.int32, (page, page), 1)
    later = (ji > si).astype(BF16)
    qbd = qbd_s[...]
    rc = rc_s[...]
    s_parts = []
    for i in range(pps):
        k_ref, _, lf_ref = page_refs[3 * i:3 * i + 3]
        lf = lf_ref[...]
        r16 = _split_dot(lf, later) + rc
        rc = rc + jnp.sum(lf, axis=-1, keepdims=True)
        kt = k_ref[...].reshape(D_MODEL, page).astype(BF16)
        s = jnp.dot(qbd, kt, preferred_element_type=F32)
        s_parts.append(s + jnp.concatenate([r16] * t_len, axis=0))
    rc_s[...] = rc
    m_old = m_s[...]
    m_new = functools.reduce(jnp.maximum, [m_old] + [jnp.max(s, axis=-1, keepdims=True) for s in s_parts])
    alpha = jnp.exp(m_old - m_new)
    l = alpha * l_s[...]
    acc = alpha * acc_s[...]
    for i, s in enumerate(s_parts):
        pr = jnp.exp(s - m_new)
        l = l + jnp.sum(pr, axis=-1, keepdims=True)
        vt = page_refs[3 * i + 1][...].reshape(D_MODEL, page).astype(BF16)
        acc = acc + lax.dot_general(pr.astype(BF16), vt, (((1,), (1,)), ((), ())),
                                    preferred_element_type=F32)
    l_s[...] = l
    acc_s[...] = acc
    m_s[...] = m_new

    @pl.when(p == pl.num_programs(1) - 1)
    def _():
        o = jnp.where(diag, acc_s[...] / l_s[...], 0.0)
        o_ref[...] = jnp.sum(o.reshape(t_len, N_HEADS, D_MODEL), axis=1)


def fox_sample_attention(q, k_new, v_new, fcum, cache_k, cache_v, cache_logf, page_table, bsz, t_len, pps=8):
    d = D_MODEL
    n_pool, page = cache_k.shape[:2]
    n_pages = page_table.shape[1]
    pps = min(pps, n_pages)
    assert n_pages % pps == 0
    nrow = t_len * N_HEADS
    q3, kn3, vn3 = (z.reshape(bsz, t_len, d) for z in (q, k_new, v_new))
    fc3 = fcum.reshape(bsz, t_len, N_HEADS)
    fnr = jnp.broadcast_to(jnp.transpose(fc3, (0, 2, 1))[:, None], (bsz, t_len, N_HEADS, t_len)).reshape(
        bsz, nrow, t_len)
    ck = jnp.transpose(cache_k, (0, 2, 3, 1))
    cv = jnp.transpose(cache_v, (0, 2, 3, 1))
    clf = jnp.transpose(cache_logf, (0, 2, 1))
    per_b = lambda shape: pl.BlockSpec((None,) + shape, lambda b, p, pt: (b,) + (0,) * len(shape))

    def paged(shape, i):
        return pl.BlockSpec((None,) + shape,
                            lambda b, p, pt: (pt[b, n_pages - 1 - (p * pps + i)],) + (0,) * len(shape))

    page_specs, page_ops = [], []
    for i in range(pps):
        page_specs += [paged((N_HEADS, HEAD_DIM, page), i), paged((N_HEADS, HEAD_DIM, page), i),
                       paged((N_HEADS, page), i)]
        page_ops += [ck, cv, clf]
    out = pl.pallas_call(
        functools.partial(_fox_sample_kernel, t_len=t_len, page=page, pps=pps),
        out_shape=jax.ShapeDtypeStruct((bsz, t_len, d), F32),
        grid_spec=pltpu.PrefetchScalarGridSpec(
            num_scalar_prefetch=1,
            grid=(bsz, n_pages // pps),
            in_specs=[per_b((t_len, d)), per_b((t_len, d)), per_b((t_len, d)), per_b((nrow, t_len))]
            + page_specs,
            out_specs=per_b((t_len, d)),
            scratch_shapes=[pltpu.VMEM((nrow, d), BF16), pltpu.VMEM((nrow, 1), F32),
                            pltpu.VMEM((nrow, 1), F32), pltpu.VMEM((nrow, d), F32),
                            pltpu.VMEM((N_HEADS, 1), F32)]),
        compiler_params=_cparams("parallel", "arbitrary"),
        name="fox_sample",
    )(page_table, q3, kn3, vn3, fnr, *page_ops)
    return out.reshape(bsz * t_len, d)


def _trunk(x3, mods, kv_mods, shift0, wkv0, attend, p, moe_w, moe_w32, depth=DEPTH):
    bsz, t_len, d = x3.shape
    x = x3.reshape(bsz * t_len, d)
    shifts, states = [], []
    v_first = None
    k_sh = v_sh = lf_sh = fcum = None
    for i in range(depth):
        precise = i < N_A and not _Rows(bsz, t_len, 256).long
        rows = _Rows(bsz, t_len, 256, precise)
        rows_moe = _Rows(bsz, t_len, 512, precise)
        ew = moe_w32 if precise else moe_w
        sh_m, sc_m, gt_m, sh_f, sc_f, gt_f = jnp.split(mods[i], 6, axis=-1)
        if i < N_A:
            vmix = None if i == 0 else (v_first, p['a_v0'][i - 1], p['a_v1'][i - 1], p['a_v2'][i - 1])
            r, w, k, v, a, g, last = rwkv_proj(
                rows, x, sh_m, sc_m, shift0[i], p['norm_g'][i, 0], p['a_mu'][i], p['a_w0'][i], p['a_a0'][i],
                p['a_wr'][i], p['a_wk'][i], p['a_wv'][i], p['a_w1'][i], p['a_w2'][i], p['a_a1'][i],
                p['a_a2'][i], p['a_g1'][i], p['a_g2'][i], vmix)
            if i == 0:
                v_first = v
            shifts.append(last)
            lanes = [_to_lanes(z, bsz, t_len) for z in (r, w, k, v, a)]
            pars = [_param_lanes(z, bsz) for z in (p['a_kk'][i], p['a_ka'][i], p['a_rk'][i].reshape(-1),
                                                   p['a_lnx_w'][i], p['a_lnx_b'][i])]
            s0 = jnp.transpose(wkv0[i], (3, 2, 0, 1)).reshape(HEAD_DIM, HEAD_DIM, bsz * N_HEADS)
            y, s_t = wkv_scan(*lanes, *pars, s0)
            states.append(jnp.transpose(s_t.reshape(HEAD_DIM, HEAD_DIM, bsz, N_HEADS), (2, 3, 1, 0)))
            x = resid_mm(rows, _from_lanes(y, bsz, t_len), g, p['a_wo'][i], x, gt_m)
        else:
            j = i - N_A
            if j == 0:
                k_sh, v_sh, lf_sh, fcum = kv_proj(rows, x, kv_mods[0], kv_mods[1], p['kv_norm_g'], p['kv_w'],
                                                  p['kv_b_f'], p['k_norm_g'])
            q = q_proj(rows, x, sh_m, sc_m, p['norm_g'][i, 0], p['b_wq'][j], p['b_q_norm_g'][j])
            o = attend(q, k_sh, v_sh, fcum)
            x = resid_mm(rows, o, None, p['b_wo'][j], x, gt_m)
        if rows_moe.rows >= SPARSE_MOE_MIN_ROWS:
            xg, grp = moe_route(rows_moe, x, sh_f, sc_f, p['norm_g'][i, 1], p['router_w'], p['router_bias'])
            y = moe_sparse(xg, grp.reshape(-1), ew[0][i], ew[1][i], ew[2][i])
            x = resid_add(rows_moe, x, y, gt_f)
        else:
            x = moe_layer(rows_moe, x, sh_f, sc_f, gt_f, p['norm_g'][i, 1], p['router_w'], p['router_bias'],
                          ew[0][i], ew[1][i], ew[2][i])
    if depth < DEPTH:
        return x.reshape(bsz, t_len, d), jnp.stack(shifts)
    hd = (bsz, t_len, N_HEADS, HEAD_DIM)
    return (x.reshape(bsz, t_len, d), jnp.stack(shifts), jnp.stack(states), k_sh.reshape(hd),
            v_sh.reshape(hd), lf_sh.reshape(bsz, t_len, N_HEADS))


def kernel(x_prompt, x_sample, c_prompt, c_sample, state_shift, state_wkv, cache_k, cache_v, cache_logf, page_table, mod_w, mod_b, norm_g, a_mu, a_w0, a_w1, a_w2, a_a0, a_a1, a_a2, a_v0, a_v1, a_v2, a_g1, a_g2, a_kk, a_ka, a_rk, a_wr, a_wk, a_wv, a_wo, a_lnx_w, a_lnx_b, kv_mod_w, kv_mod_b, kv_norm_g, kv_w, kv_b_f, k_norm_g, b_wq, b_q_norm_g, b_wo, router_w, router_bias, moe_w_gate, moe_w_up, moe_w_down):
    p = dict(norm_g=norm_g, a_mu=a_mu, a_w0=a_w0, a_w1=a_w1, a_w2=a_w2, a_a0=a_a0, a_a1=a_a1, a_a2=a_a2,
             a_v0=a_v0, a_v1=a_v1, a_v2=a_v2, a_g1=a_g1, a_g2=a_g2, a_kk=a_kk, a_ka=a_ka, a_rk=a_rk,
             a_wr=a_wr, a_wk=a_wk, a_wv=a_wv, a_wo=a_wo, a_lnx_w=a_lnx_w, a_lnx_b=a_lnx_b,
             kv_norm_g=kv_norm_g, kv_w=kv_w, kv_b_f=kv_b_f, k_norm_g=k_norm_g, b_wq=b_wq,
             b_q_norm_g=b_q_norm_g, b_wo=b_wo, router_w=router_w, router_bias=router_bias)
    d = D_MODEL
    bp, t_p = x_prompt.shape[:2]
    bs, t_s = x_sample.shape[:2]
    moe_w = (moe_w_gate.astype(BF16), moe_w_up.astype(BF16), moe_w_down.astype(BF16))

    w_all = jnp.concatenate([mod_w[i] for i in range(DEPTH)] + [kv_mod_w], axis=1)
    b_all = jnp.concatenate([mod_b[i] for i in range(DEPTH)] + [kv_mod_b], axis=0)
    m_all = adaln_all(jnp.concatenate([c_prompt, c_sample], axis=0), w_all, b_all)

    def split_mods(m):
        layer = [m[:, i * 6 * d:(i + 1) * 6 * d] for i in range(DEPTH)]
        kvm = m[:, DEPTH * 6 * d:]
        return layer, (kvm[:, :d], kvm[:, d:])

    mods_p, kvm_p = split_mods(m_all[:bp])
    mods_s, kvm_s = split_mods(m_all[bp:])

    def attend_prompt(q, k, v, fcum):
        return fox_prompt_attention(q, k, v, fcum, bp, t_p)

    def attend_sample(q, k, v, fcum):
        return fox_sample_attention(q, k, v, fcum, cache_k, cache_v, cache_logf, page_table, bs, t_s)

    shift0_p = jnp.zeros((N_A, bp, d), F32)
    wkv0_p = jnp.zeros((N_A, bp, N_HEADS, HEAD_DIM, HEAD_DIM), F32)
    moe_w32 = (moe_w_gate, moe_w_up, moe_w_down)
    y_p, shift_p, wkv_p, k_p, v_p, lf_p = _trunk(x_prompt, mods_p, kvm_p, shift0_p, wkv0_p, attend_prompt,
                                                 p, moe_w, moe_w32)
    y_s, shift_s, wkv_s, k_s, v_s, lf_s = _trunk(x_sample, mods_s, kvm_s, state_shift, state_wkv,
                                                 attend_sample, p, moe_w, moe_w32)
    return (y_p, y_s, shift_p, wkv_p, k_p, v_p, lf_p, shift_s, wkv_s, k_s, v_s, lf_s)
```

```python
import functools

import jax
import jax.numpy as jnp
from jax import lax
from jax.experimental import pallas as pl
from jax.experimental.pallas import tpu as pltpu

D_MODEL = 1024
HEAD_DIM = 64
N_HEADS = D_MODEL // HEAD_DIM
N_EXPERTS = 16
N_GROUPS = 4
EXPERTS_PER_GROUP = N_EXPERTS // N_GROUPS
D_EXPERT = 512
DEPTH = 4
N_A = 2
RMS_EPS = 1e-6
GN_EPS = 64e-5
LANES = 128
VMEM_LIMIT = 56 * 1024 * 1024
SPARSE_MOE_MIN_ROWS = 4096

F32 = jnp.float32
BF16 = jnp.bfloat16
NEG_INF = float("-inf")


def _cparams(*sem):
    return pltpu.CompilerParams(dimension_semantics=sem, vmem_limit_bytes=VMEM_LIMIT)


def _bdot(a, b):
    return jnp.dot(a.astype(BF16), b.astype(BF16), preferred_element_type=F32)


def _dot3(a, b):
    a = a.astype(F32)
    b = b.astype(F32)
    a1 = a.astype(BF16)
    a2 = (a - a1.astype(F32)).astype(BF16)
    b1 = b.astype(BF16)
    b2 = (b - b1.astype(F32)).astype(BF16)
    return (jnp.dot(a1, b1, preferred_element_type=F32)
            + (jnp.dot(a1, b2, preferred_element_type=F32) + jnp.dot(a2, b1, preferred_element_type=F32)))


def _mm(a, b, precise):
    return _dot3(a, b) if precise else _bdot(a, b)


def _split_dot(a, m):
    a1 = a.astype(BF16)
    r1 = a - a1.astype(F32)
    a2 = r1.astype(BF16)
    a3 = (r1 - a2.astype(F32)).astype(BF16)
    return (jnp.dot(a1, m, preferred_element_type=F32) + jnp.dot(a2, m, preferred_element_type=F32)
            + jnp.dot(a3, m, preferred_element_type=F32))


def _sigmoid(z):
    return 1.0 / (1.0 + jnp.exp(-z))


def _silu(z):
    return z * _sigmoid(z)


def _softplus(z):
    return jnp.maximum(z, 0.0) + jnp.log(1.0 + jnp.exp(-jnp.abs(z)))


def _modulate(x, g, sh, sc):
    y = x * lax.rsqrt(jnp.mean(x * x, axis=-1, keepdims=True) + RMS_EPS)
    return (y * g) * (1.0 + sc) + sh


def _head_rms(z, gsum, gexp, g_row):
    ss = _split_dot(z * z, gsum)
    inv = lax.rsqrt(ss * (1.0 / HEAD_DIM) + RMS_EPS)
    return z * _split_dot(inv, gexp) * g_row


def _group_mats():
    head_of_lane = jnp.arange(D_MODEL) // HEAD_DIM
    gsum = (head_of_lane[:, None] == jnp.arange(LANES)[None, :]).astype(BF16)
    return gsum, gsum.T


class _Rows:
    def __init__(self, bsz, t_len, tile, precise=False):
        self.bsz, self.t_len = bsz, t_len
        self.precise = precise
        self.rows = bsz * t_len
        self.tm = min(tile, self.rows)
        self.long = t_len >= self.tm
        if self.long:
            assert t_len % self.tm == 0
        else:
            assert self.tm % t_len == 0
        assert self.rows % self.tm == 0
        self.n_tiles = self.rows // self.tm
        self.tiles_per_seq = t_len // self.tm if self.long else 1

    def row_spec(self, width=D_MODEL):
        return pl.BlockSpec((self.tm, width), lambda i, *_: (i, 0))

    def mod_operand(self, m):
        w = m.shape[-1]
        if self.long:
            tps = self.tiles_per_seq
            return m.reshape(self.bsz, 1, w), pl.BlockSpec((None, 1, w), lambda i, *_: (i // tps, 0, 0))
        return jnp.repeat(m, self.t_len, axis=0), pl.BlockSpec((self.tm, w), lambda i, *_: (i, 0))


def _full_spec(a):
    nd = a.ndim
    return pl.BlockSpec(a.shape, lambda *_: (0,) * nd)


def _adaln_kernel(c_ref, w_ref, b_ref, o_ref):
    c = c_ref[...]
    o_ref[...] = jnp.dot(_silu(c), w_ref[...], precision=lax.Precision.HIGHEST,
                         preferred_element_type=F32) + b_ref[...]


def adaln_all(c, w, b, tn=1024):
    m, k = c.shape
    n = w.shape[1]
    return pl.pallas_call(
        _adaln_kernel,
        out_shape=jax.ShapeDtypeStruct((m, n), F32),
        grid=(n // tn,),
        in_specs=[pl.BlockSpec((m, k), lambda j: (0, 0)), pl.BlockSpec((k, tn), lambda j: (0, j)),
                  pl.BlockSpec((1, tn), lambda j: (0, j))],
        out_specs=pl.BlockSpec((m, tn), lambda j: (0, j)),
        compiler_params=_cparams("parallel"),
        name="adaln",
    )(c, w, b.reshape(1, n))


def _rwkv_proj_kernel(*refs, rows, has_vmix):
    (x_ref, sh_ref, sc_ref, first_ref, g_ref, mu_ref, w0_ref, a0_ref,
     wr_ref, wk_ref, wv_ref, w1_ref, w2_ref, a1_ref, a2_ref, g1_ref, g2_ref) = refs[:17]
    pos = 17
    if has_vmix:
        vf_ref, v0_ref, v1_ref, v2_ref = refs[pos:pos + 4]
        pos += 4
    r_ref, w_ref, k_ref, v_ref, a_ref, gg_ref, last_ref = refs[pos:pos + 7]
    carry = refs[pos + 7]
    tm = rows.tm
    i = pl.program_id(0)

    xn = _modulate(x_ref[...], g_ref[...], sh_ref[...], sc_ref[...])
    rolled = pltpu.roll(xn, 1, 0)
    row = lax.broadcasted_iota(jnp.int32, (tm, 1), 0)
    if rows.long:
        @pl.when(i % rows.tiles_per_seq == 0)
        def _():
            carry[...] = first_ref[...]
        x_prev = jnp.where(row == 0, carry[...], rolled)
        carry[...] = xn[tm - 1:tm, :]
        last_ref[...] = xn[tm - 1:tm, :]
    else:
        x_prev = jnp.where(row % rows.t_len == 0, first_ref[...], rolled)
        last_ref[...] = xn
    xx = x_prev - xn
    mu = mu_ref[...]

    mm = functools.partial(_mm, precise=rows.precise)

    def mix(j):
        z = xn + xx * mu[j:j + 1, :]
        return z if rows.precise else z.astype(BF16)

    xv = mix(3)
    r_ref[...] = mm(mix(0), wr_ref[...])
    w_ref[...] = -_softplus(-(w0_ref[...] + mm(jnp.tanh(mm(mix(1), w1_ref[...])), w2_ref[...]))) - 0.5
    k_ref[...] = mm(mix(2), wk_ref[...])
    v = mm(xv, wv_ref[...])
    if has_vmix:
        v = v + (vf_ref[...] - v) * _sigmoid(v0_ref[...] + mm(mm(xv, v1_ref[...]), v2_ref[...]))
    v_ref[...] = v
    a_ref[...] = _sigmoid(a0_ref[...] + mm(mm(mix(4), a1_ref[...]), a2_ref[...]))
    gg_ref[...] = mm(_sigmoid(mm(mix(5), g1_ref[...])), g2_ref[...])


def rwkv_proj(rows, x, sh, sc, shift0, norm_g, mu, w0, a0, wr, wk, wv, w1, w2, a1, a2, g1, g2, vmix):
    d = D_MODEL
    sh_op, sh_spec = rows.mod_operand(sh)
    sc_op, sc_spec = rows.mod_operand(sc)
    first_op, first_spec = rows.mod_operand(shift0)
    row2 = lambda z: z.reshape(1, -1)
    wcast = (lambda z: z) if rows.precise else (lambda z: z.astype(BF16))
    ops = [x, sh_op, sc_op, first_op, row2(norm_g), mu, row2(w0), row2(a0)] + [
        wcast(z) for z in (wr, wk, wv, w1, w2, a1, a2, g1, g2)]
    specs = [rows.row_spec(), sh_spec, sc_spec, first_spec] + [_full_spec(o) for o in ops[4:]]
    if vmix is not None:
        v_first, v0, v1, v2 = vmix
        extra = [v_first, row2(v0), wcast(v1), wcast(v2)]
        ops += extra
        specs += [rows.row_spec()] + [_full_spec(o) for o in extra[1:]]
    act = jax.ShapeDtypeStruct((rows.rows, d), F32)
    if rows.long:
        tps = rows.tiles_per_seq
        last_shape = jax.ShapeDtypeStruct((rows.bsz, 1, d), F32)
        last_spec = pl.BlockSpec((None, 1, d), lambda i: (i // tps, 0, 0))
    else:
        last_shape, last_spec = act, rows.row_spec()
    outs = pl.pallas_call(
        functools.partial(_rwkv_proj_kernel, rows=rows, has_vmix=vmix is not None),
        out_shape=(act,) * 6 + (last_shape,),
        grid=(rows.n_tiles,),
        in_specs=specs,
        out_specs=(rows.row_spec(),) * 6 + (last_spec,),
        scratch_shapes=[pltpu.VMEM((1, d), F32)],
        compiler_params=_cparams("arbitrary"),
        name="rwkv_proj",
    )(*ops)
    last = outs[6]
    last = last.reshape(rows.bsz, d) if rows.long else last.reshape(rows.bsz, rows.t_len, d)[:, -1]
    return outs[:6] + (last,)


def _wkv_kernel(r_ref, w_ref, k_ref, v_ref, a_ref, kk_ref, ka_ref, rk_ref, lw_ref, lb_ref, s0_ref,
                y_ref, st_ref, S, nk_s, bb_s, km_s, dc_s, y_s, *, tt):
    tb = pl.program_id(1)

    @pl.when(tb == 0)
    def _():
        S[...] = s0_ref[...]

    k = k_ref[...]
    a = a_ref[...]
    kkr = k * kk_ref[...][None]
    ss = jnp.sum(kkr * kkr, axis=1, keepdims=True)
    kk = kkr * lax.rsqrt(jnp.maximum(ss, 1e-24))
    km = k * (1.0 + (a - 1.0) * ka_ref[...][None])
    nk_s[...] = -kk
    bb_s[...] = kk * a
    km_s[...] = km
    dc_s[...] = jnp.exp(-jnp.exp(w_ref[...]))

    def step(t, carry):
        def row(ref, j):
            return jnp.broadcast_to(ref[t, pl.ds(j, 1), :], (HEAD_DIM, LANES))

        sa = S[0] * row(nk_s, 0)
        for j in range(1, HEAD_DIM):
            sa = sa + S[j] * row(nk_s, j)
        vt = v_ref[t]
        y = None
        for j in range(HEAD_DIM):
            s_new = S[j] * row(dc_s, j) + sa * row(bb_s, j) + vt * row(km_s, j)
            S[j] = s_new
            yj = s_new * row(r_ref, j)
            y = yj if y is None else y + yj
        y_s[t] = y
        return carry

    lax.fori_loop(0, tt, step, 0)

    y = y_s[...]
    mean = jnp.mean(y, axis=1, keepdims=True)
    yc = y - mean
    var = jnp.mean(yc * yc, axis=1, keepdims=True)
    bonus = jnp.sum(r_ref[...] * km * rk_ref[...][None], axis=1, keepdims=True) * v_ref[...]
    y_ref[...] = yc * lax.rsqrt(var + GN_EPS) * lw_ref[...][None] + lb_ref[...][None] + bonus

    @pl.when(tb == pl.num_programs(1) - 1)
    def _():
        st_ref[...] = S[...]


def wkv_scan(r, w, k, v, a, kk_p, ka_p, rk_p, lw_p, lb_p, s0, *, tt=16):
    t_len, n, l_tot = r.shape
    tt = min(tt, t_len)
    assert t_len % tt == 0 and l_tot % LANES == 0
    seq_spec = pl.BlockSpec((tt, n, LANES), lambda g, t: (t, 0, g))
    par_spec = pl.BlockSpec((n, LANES), lambda g, t: (0, g))
    st_spec = pl.BlockSpec((n, n, LANES), lambda g, t: (0, 0, g))
    return pl.pallas_call(
        functools.partial(_wkv_kernel, tt=tt),
        out_shape=(jax.ShapeDtypeStruct(r.shape, F32), jax.ShapeDtypeStruct(s0.shape, F32)),
        grid=(l_tot // LANES, t_len // tt),
        in_specs=[seq_spec] * 5 + [par_spec] * 5 + [st_spec],
        out_specs=(seq_spec, st_spec),
        scratch_shapes=[pltpu.VMEM((n, n, LANES), F32)] + [pltpu.VMEM((tt, n, LANES), F32)] * 5,
        compiler_params=_cparams("parallel", "arbitrary"),
        name="wkv_scan",
    )(r, w, k, v, a, kk_p, ka_p, rk_p, lw_p, lb_p, s0)


LANE_BATCH = LANES // N_HEADS
CHANNEL_ORDER = tuple(int(h * HEAD_DIM + n) for n in range(HEAD_DIM) for h in range(N_HEADS))


def _wkv_rows_kernel(r_ref, w_ref, k_ref, v_ref, a_ref, kk_ref, ka_ref, rk_ref, lw_ref, lb_ref, s0_ref,
                     y_ref, st_ref, S, r_s, dc_s, km_s, v_s, bb_s, nk_s, y_s, *, tt):
    tb = pl.program_id(1)
    n = HEAD_DIM
    n_chunks = D_MODEL // LANES
    per_chunk = LANES // N_HEADS

    @pl.when(tb == 0)
    def _():
        S[...] = s0_ref[...]

    def to_lanes(src, dst):
        for c in range(n_chunks):
            xt = [src[b, :, c * LANES:(c + 1) * LANES].T for b in range(LANE_BATCH)]
            for i in range(per_chunk):
                m = jnp.concatenate([x[i * N_HEADS:(i + 1) * N_HEADS, :] for x in xt], axis=0)
                dst[pl.ds(c * per_chunk + i, tt, stride=n), :] = m.T

    def from_lanes(src, dst):
        for c in range(n_chunks):
            parts = [src[pl.ds(c * per_chunk + i, tt, stride=n), :].T for i in range(per_chunk)]
            for b in range(LANE_BATCH):
                xb = jnp.concatenate([p[b * N_HEADS:(b + 1) * N_HEADS, :] for p in parts], axis=0)
                dst[b, :, c * LANES:(c + 1) * LANES] = xb.T

    to_lanes(r_ref, r_s)
    to_lanes(w_ref, dc_s)
    to_lanes(k_ref, km_s)
    to_lanes(v_ref, v_s)
    to_lanes(a_ref, bb_s)

    def cube(z):
        return z.reshape(tt, n, LANES)

    def flat(z):
        return z.reshape(tt * n, LANES)

    k = cube(km_s[...])
    a = cube(bb_s[...])
    kkr = k * kk_ref[...][None]
    ss = jnp.sum(kkr * kkr, axis=1, keepdims=True)
    kk = kkr * lax.rsqrt(jnp.maximum(ss, 1e-24))
    km = k * (1.0 + (a - 1.0) * ka_ref[...][None])
    nk_s[...] = flat(-kk)
    bb_s[...] = flat(kk * a)
    km_s[...] = flat(km)
    dc_s[...] = jnp.exp(-jnp.exp(dc_s[...]))

    def step(t, carry):
        base = pl.multiple_of(t * n, n)

        def row(ref, j):
            return jnp.broadcast_to(ref[pl.ds(base + j, 1), :], (n, LANES))

        sa = S[0] * row(nk_s, 0)
        for j in range(1, n):
            sa = sa + S[j] * row(nk_s, j)
        vt = v_s[pl.ds(base, n), :]
        y = None
        for j in range(n):
            s_new = S[j] * row(dc_s, j) + sa * row(bb_s, j) + vt * row(km_s, j)
            S[j] = s_new
            yj = s_new * row(r_s, j)
            y = yj if y is None else y + yj
        y_s[pl.ds(base, n), :] = y
        return carry

    lax.fori_loop(0, tt, step, 0)

    y = cube(y_s[...])
    mean = jnp.mean(y, axis=1, keepdims=True)
    yc = y - mean
    var = jnp.mean(yc * yc, axis=1, keepdims=True)
    bonus = jnp.sum(cube(r_s[...]) * cube(km_s[...]) * rk_ref[...][None], axis=1, keepdims=True) * cube(v_s[...])
    y_s[...] = flat(yc * lax.rsqrt(var + GN_EPS) * lw_ref[...][None] + lb_ref[...][None] + bonus)
    from_lanes(y_s, y_ref)

    @pl.when(tb == pl.num_programs(1) - 1)
    def _():
        st_ref[...] = S[...]


def wkv_scan_rows(r, w, k, v, a, kk_p, ka_p, rk_p, lw_p, lb_p, s0, *, tt=64):
    bsz, t_len, d = r.shape
    n = HEAD_DIM
    assert bsz % LANE_BATCH == 0 and t_len % tt == 0
    seq_spec = pl.BlockSpec((LANE_BATCH, tt, d), lambda g, t: (g, t, 0))
    par_spec = pl.BlockSpec((n, LANES), lambda g, t: (0, g))
    st_spec = pl.BlockSpec((n, n, LANES), lambda g, t: (0, 0, g))
    return pl.pallas_call(
        functools.partial(_wkv_rows_kernel, tt=tt),
        out_shape=(jax.ShapeDtypeStruct(r.shape, F32), jax.ShapeDtypeStruct(s0.shape, F32)),
        grid=(bsz // LANE_BATCH, t_len // tt),
        in_specs=[seq_spec] * 5 + [par_spec] * 5 + [st_spec],
        out_specs=(seq_spec, st_spec),
        scratch_shapes=[pltpu.VMEM((n, n, LANES), F32)] + [pltpu.VMEM((tt * n, LANES), F32)] * 7,
        compiler_params=_cparams("parallel", "arbitrary"),
        name="wkv_scan_rows",
    )(r, w, k, v, a, kk_p, ka_p, rk_p, lw_p, lb_p, s0)


def _to_lanes(z, bsz, t_len):
    return jnp.transpose(z.reshape(bsz, t_len, N_HEADS, HEAD_DIM), (1, 3, 0, 2)).reshape(
        t_len, HEAD_DIM, bsz * N_HEADS)


def _from_lanes(z, bsz, t_len):
    return jnp.transpose(z.reshape(t_len, HEAD_DIM, bsz, N_HEADS), (2, 0, 3, 1)).reshape(
        bsz * t_len, D_MODEL)


def _param_lanes(p, bsz):
    return jnp.broadcast_to(p.reshape(1, N_HEADS, HEAD_DIM), (bsz, N_HEADS, HEAD_DIM)).transpose(
        2, 0, 1).reshape(HEAD_DIM, bsz * N_HEADS)


def _resid_mm_kernel(*refs, has_mul, precise):
    if has_mul:
        a_ref, b_ref, w_ref, x_ref, gt_ref, o_ref = refs
        lhs = a_ref[...] * b_ref[...]
    else:
        a_ref, w_ref, x_ref, gt_ref, o_ref = refs
        lhs = a_ref[...]
    o_ref[...] = x_ref[...] + gt_ref[...] * _mm(lhs, w_ref[...], precise)


def resid_mm(rows, a, b, w, x, gate):
    gt_op, gt_spec = rows.mod_operand(gate)
    w = w if rows.precise else w.astype(BF16)
    ops = [a] + ([b] if b is not None else []) + [w, x, gt_op]
    specs = [rows.row_spec()] * (2 if b is not None else 1) + [_full_spec(w), rows.row_spec(), gt_spec]
    return pl.pallas_call(
        functools.partial(_resid_mm_kernel, has_mul=b is not None, precise=rows.precise),
        out_shape=jax.ShapeDtypeStruct(x.shape, F32),
        grid=(rows.n_tiles,),
        in_specs=specs,
        out_specs=rows.row_spec(),
        compiler_params=_cparams("parallel"),
        name="resid_mm",
    )(*ops)


def _route(probs, bias):
    tm = probs.shape[0]
    sel = probs + bias
    lane = lax.broadcasted_iota(jnp.int32, (tm, N_EXPERTS), 1)
    grp = lane // EXPERTS_PER_GROUP

    def top2(mask):
        x = jnp.where(mask, sel, NEG_INF)
        v1 = jnp.max(x, axis=-1, keepdims=True)
        i1 = jnp.min(jnp.where(x == v1, lane, N_EXPERTS), axis=-1, keepdims=True)
        x2 = jnp.where(lane == i1, NEG_INF, x)
        v2 = jnp.max(x2, axis=-1, keepdims=True)
        i2 = jnp.min(jnp.where(x2 == v2, lane, N_EXPERTS), axis=-1, keepdims=True)
        return v1 + v2, i1, i2

    best_score, _, _ = top2(grp == 0)
    best = jnp.zeros((tm, 1), jnp.int32)
    for g in range(1, N_GROUPS):
        score, _, _ = top2(grp == g)
        better = score > best_score
        best = jnp.where(better, g, best)
        best_score = jnp.where(better, score, best_score)
    _, i1, i2 = top2(grp == best)
    chosen = (lane == i1) | (lane == i2)
    w = jnp.where(chosen, probs, 0.0)
    return w / jnp.sum(w, axis=-1, keepdims=True), best


def _router_gate(xn, rw, rb):
    logits = jnp.dot(xn, rw, precision=lax.Precision.HIGHEST, preferred_element_type=F32)
    z = jnp.exp(logits - jnp.max(logits, axis=-1, keepdims=True))
    return _route(z / jnp.sum(z, axis=-1, keepdims=True), rb)


def _moe_kernel(x_ref, sh_ref, sc_ref, gt_ref, g_ref, rw_ref, rb_ref, wg_ref, wu_ref, wd_ref,
                o_ref, xn_s, gate_s, acc_s, *, precise):
    e = pl.program_id(1)

    @pl.when(e == 0)
    def _():
        xn = _modulate(x_ref[...], g_ref[...], sh_ref[...], sc_ref[...])
        xn_s[...] = xn.astype(xn_s.dtype)
        gate_s[...], _ = _router_gate(xn, rw_ref[...], rb_ref[...])
        acc_s[...] = jnp.zeros_like(acc_s)

    acc_s[...] += _expert_ffn(xn_s[...], gate_s[...], e, wg_ref, wu_ref, wd_ref, precise)

    @pl.when(e == pl.num_programs(1) - 1)
    def _():
        o_ref[...] = x_ref[...] + gt_ref[...] * acc_s[...]


def _expert_ffn(xn, gate, expert, wg_ref, wu_ref, wd_ref, precise=False):
    lane = lax.broadcasted_iota(jnp.int32, gate.shape, 1)
    g_e = jnp.sum(jnp.where(lane == expert, gate, 0.0), axis=-1, keepdims=True)
    h = _silu(_mm(xn, wg_ref[...], precise)) * _mm(xn, wu_ref[...], precise)
    return _mm(h * g_e, wd_ref[...], precise)


def moe_layer(rows, x, sh, sc, gate, norm_g, router_w, router_bias, w_gate, w_up, w_down, layer):
    d = D_MODEL
    sh_op, sh_spec = rows.mod_operand(sh)
    sc_op, sc_spec = rows.mod_operand(sc)
    gt_op, gt_spec = rows.mod_operand(gate)
    g2 = norm_g.reshape(1, d)
    rb = router_bias.reshape(1, N_EXPERTS)
    tm = rows.tm
    return pl.pallas_call(
        functools.partial(_moe_kernel, precise=rows.precise),
        out_shape=jax.ShapeDtypeStruct(x.shape, F32),
        grid=(rows.n_tiles, N_EXPERTS),
        in_specs=[rows.row_spec(), sh_spec, sc_spec, gt_spec, _full_spec(g2), _full_spec(router_w),
                  _full_spec(rb),
                  pl.BlockSpec((None, None, d, D_EXPERT), lambda i, e: (layer, e, 0, 0)),
                  pl.BlockSpec((None, None, d, D_EXPERT), lambda i, e: (layer, e, 0, 0)),
                  pl.BlockSpec((None, None, D_EXPERT, d), lambda i, e: (layer, e, 0, 0))],
        out_specs=rows.row_spec(),
        scratch_shapes=[pltpu.VMEM((tm, d), F32 if rows.precise else BF16),
                        pltpu.VMEM((tm, N_EXPERTS), F32), pltpu.VMEM((tm, d), F32)],
        compiler_params=_cparams("parallel", "arbitrary"),
        name="moe",
    )(x, sh_op, sc_op, gt_op, g2, router_w, rb, w_gate, w_up, w_down)


PAYLOAD = D_MODEL + LANES


def _moe_route_kernel(x_ref, sh_ref, sc_ref, g_ref, rw_ref, rb_ref, xg_ref, grp_ref):
    xn = _modulate(x_ref[...], g_ref[...], sh_ref[...], sc_ref[...])
    gate, best = _router_gate(xn, rw_ref[...], rb_ref[...])
    xg_ref[:, :D_MODEL] = xn
    xg_ref[:, D_MODEL:] = jnp.zeros((xn.shape[0], LANES), F32)
    xg_ref[:, D_MODEL:D_MODEL + N_EXPERTS] = gate
    grp_ref[...] = best


def moe_route(rows, x, sh, sc, norm_g, router_w, router_bias):
    d = D_MODEL
    sh_op, sh_spec = rows.mod_operand(sh)
    sc_op, sc_spec = rows.mod_operand(sc)
    g2 = norm_g.reshape(1, d)
    rb = router_bias.reshape(1, N_EXPERTS)
    return pl.pallas_call(
        _moe_route_kernel,
        out_shape=(jax.ShapeDtypeStruct((rows.rows, PAYLOAD), F32),
                   jax.ShapeDtypeStruct((rows.rows, 1), jnp.int32)),
        grid=(rows.n_tiles,),
        in_specs=[rows.row_spec(), sh_spec, sc_spec, _full_spec(g2), _full_spec(router_w), _full_spec(rb)],
        out_specs=(rows.row_spec(PAYLOAD), rows.row_spec(1)),
        compiler_params=_cparams("parallel"),
        name="moe_route",
    )(x, sh_op, sc_op, g2, router_w, rb)


def _group_order(grp, tm):
    n = grp.shape[0]
    p_slots = n + N_GROUPS * tm
    onehot = (grp[:, None] == jnp.arange(N_GROUPS, dtype=jnp.int32)[None, :]).astype(jnp.int32)
    counts = jnp.sum(onehot, axis=0)
    rank = jnp.sum((jnp.cumsum(onehot, axis=0) - onehot) * onehot, axis=1)
    padded = ((counts + tm - 1) // tm) * tm
    ends = jnp.cumsum(padded)
    pos = jnp.sum(onehot * (ends - padded)[None, :], axis=1) + rank
    token = jnp.arange(n, dtype=jnp.int32)
    filled = jnp.zeros((p_slots + 2 * tm,), jnp.int32).at[pos].set(token + 1)
    src = jnp.maximum(filled - 1, 0)
    real = jnp.minimum(filled[:p_slots], 1)
    dst = jnp.where(real == 1, src[:p_slots], n + jnp.cumsum(1 - real) - 1).astype(jnp.int32)
    dst_prev = jnp.concatenate([p_slots + jnp.arange(tm, dtype=jnp.int32), dst])
    tile_start = jnp.arange(p_slots // tm + 1, dtype=jnp.int32) * tm
    tile_grp = jnp.sum((tile_start[:, None] >= ends[None, :]).astype(jnp.int32), axis=1)
    tile_grp = jnp.where(tile_grp < N_GROUPS, tile_grp, -1).astype(jnp.int32)
    return src, dst_prev, tile_grp


def _moe_sparse_kernel(src_ref, dst_ref, tg_ref, xg_hbm, wg_ref, wu_ref, wd_ref, y_hbm,
                       xbuf, ybuf, xn_s, gate_s, acc_s, gsem, ssem, *, tm):
    i = pl.program_id(0)
    e = pl.program_id(1)
    nt = pl.num_programs(0)
    slot = i % 2
    prev = 1 - slot
    quarter = tm // EXPERTS_PER_GROUP

    def row_in(tile, to_slot, k):
        return pltpu.make_async_copy(xg_hbm.at[pl.ds(src_ref[tile * tm + k], 1), :],
                                     xbuf.at[to_slot, pl.ds(k, 1), :], gsem.at[to_slot])

    def row_out(tile, from_slot, k):
        return pltpu.make_async_copy(ybuf.at[from_slot, pl.ds(k, 1), :],
                                     y_hbm.at[pl.ds(dst_ref[tile * tm + k], 1), :], ssem.at[from_slot])

    def wait_in(slot_):
        pltpu.make_async_copy(xg_hbm.at[pl.ds(0, tm), :], xbuf.at[slot_], gsem.at[slot_]).wait()

    def wait_out(slot_):
        pltpu.make_async_copy(ybuf.at[slot_], y_hbm.at[pl.ds(0, tm), :], ssem.at[slot_]).wait()

    @pl.when((i == 0) & (e == 0))
    def _():
        def body(k, c):
            row_in(0, 0, k).start()
            return c
        lax.fori_loop(0, tm, body, 0)
        ybuf[1] = jnp.zeros((tm, D_MODEL), F32)

    @pl.when(e == 0)
    def _():
        wait_in(slot)
        xn_s[...] = xbuf[slot, :, :D_MODEL].astype(BF16)
        gate_s[...] = xbuf[slot, :, D_MODEL:]

    def start_neighbour_rows():
        for kk in range(quarter):
            k = e * quarter + kk
            row_in(i + 1, prev, k).start()
            row_out(i, prev, k).start()

    grp = tg_ref[i]

    @pl.when(grp >= 0)
    def _():
        start_neighbour_rows()
        contrib = _expert_ffn(xn_s[...], gate_s[...], grp * EXPERTS_PER_GROUP + e, wg_ref, wu_ref, wd_ref)

        @pl.when(e == 0)
        def _():
            acc_s[...] = contrib

        @pl.when(e > 0)
        def _():
            acc_s[...] += contrib

    @pl.when(grp < 0)
    def _():
        start_neighbour_rows()

    @pl.when(e == EXPERTS_PER_GROUP - 1)
    def _():
        @pl.when(i >= 1)
        def _():
            wait_out(slot)

        @pl.when(grp >= 0)
        def _():
            ybuf[slot] = acc_s[...]

        @pl.when(i == nt - 1)
        def _():
            wait_out(prev)
            wait_in(prev)


def moe_sparse(xg, grp, w_gate, w_up, w_down, layer, tm=512):
    d = D_MODEL
    src, dst_prev, tile_grp = _group_order(grp, tm)
    n_tiles = tile_grp.shape[0]
    expert_of = lambda i, e, s, t, tg: (layer, jnp.maximum(tg[i], 0) * EXPERTS_PER_GROUP + e, 0, 0)
    return pl.pallas_call(
        functools.partial(_moe_sparse_kernel, tm=tm),
        out_shape=jax.ShapeDtypeStruct((n_tiles * tm, d), F32),
        grid_spec=pltpu.PrefetchScalarGridSpec(
            num_scalar_prefetch=3,
            grid=(n_tiles, EXPERTS_PER_GROUP),
            in_specs=[pl.BlockSpec(memory_space=pl.ANY),
                      pl.BlockSpec((None, None, d, D_EXPERT), expert_of),
                      pl.BlockSpec((None, None, d, D_EXPERT), expert_of),
                      pl.BlockSpec((None, None, D_EXPERT, d), expert_of)],
            out_specs=pl.BlockSpec(memory_space=pl.ANY),
            scratch_shapes=[pltpu.VMEM((2, tm, PAYLOAD), F32), pltpu.VMEM((2, tm, d), F32),
                            pltpu.VMEM((tm, d), BF16), pltpu.VMEM((tm, LANES), F32),
                            pltpu.VMEM((tm, d), F32),
                            pltpu.SemaphoreType.DMA((2,)), pltpu.SemaphoreType.DMA((2,))]),
        compiler_params=_cparams("arbitrary", "arbitrary"),
        name="moe_sparse",
    )(src, dst_prev, tile_grp, xg, w_gate, w_up, w_down)


def _resid_add_kernel(x_ref, y_ref, gt_ref, o_ref):
    o_ref[...] = x_ref[...] + gt_ref[...] * y_ref[...]


def resid_add(rows, x, y, gate):
    gt_op, gt_spec = rows.mod_operand(gate)
    return pl.pallas_call(
        _resid_add_kernel,
        out_shape=jax.ShapeDtypeStruct(x.shape, F32),
        grid=(rows.n_tiles,),
        in_specs=[rows.row_spec(), rows.row_spec(), gt_spec],
        out_specs=rows.row_spec(),
        compiler_params=_cparams("parallel"),
        name="resid_add",
    )(x, y, gt_op)


def _kv_proj_kernel(x_ref, sh_ref, sc_ref, g_ref, wk_ref, wv_ref, wf_ref, bf_ref, kg_ref, gsum_ref,
                    gexp_ref, k_ref, v_ref, lf_ref, fc_ref, carry, *, rows):
    tm = rows.tm
    i = pl.program_id(0)
    xn = _modulate(x_ref[...], g_ref[...], sh_ref[...], sc_ref[...]).astype(BF16)
    k_ref[...] = _head_rms(jnp.dot(xn, wk_ref[...], preferred_element_type=F32), gsum_ref[...],
                           gexp_ref[...], kg_ref[...])
    v_ref[...] = jnp.dot(xn, wv_ref[...], preferred_element_type=F32)
    z = jnp.dot(xn, wf_ref[...], preferred_element_type=F32) + bf_ref[...]
    lf = -_softplus(-z)
    lf_ref[...] = lf[:, :N_HEADS]
    ri = lax.broadcasted_iota(jnp.int32, (tm, tm), 0)
    ci = lax.broadcasted_iota(jnp.int32, (tm, tm), 1)
    if rows.long:
        tri = (ci <= ri).astype(BF16)

        @pl.when(i % rows.tiles_per_seq == 0)
        def _():
            carry[...] = jnp.zeros_like(carry)
        cum = _split_dot_lhs_exact(tri, lf) + carry[...]
        carry[...] = cum[tm - 1:tm, :]
    else:
        tri = ((ci <= ri) & (ci // rows.t_len == ri // rows.t_len)).astype(BF16)
        cum = _split_dot_lhs_exact(tri, lf)
    fc_ref[...] = cum[:, :N_HEADS]


def _split_dot_lhs_exact(m, b):
    b1 = b.astype(BF16)
    r1 = b - b1.astype(F32)
    b2 = r1.astype(BF16)
    b3 = (r1 - b2.astype(F32)).astype(BF16)
    return (jnp.dot(m, b1, preferred_element_type=F32) + jnp.dot(m, b2, preferred_element_type=F32)
            + jnp.dot(m, b3, preferred_element_type=F32))


def kv_proj(rows, x, sh, sc, norm_g, kv_w, kv_b_f, k_norm_g):
    d = D_MODEL
    sh_op, sh_spec = rows.mod_operand(sh)
    sc_op, sc_spec = rows.mod_operand(sc)
    gsum, gexp = _group_mats()
    wk = kv_w[:, :d].astype(BF16)
    wv = kv_w[:, d:2 * d].astype(BF16)
    wf = jnp.pad(kv_w[:, 2 * d:], ((0, 0), (0, LANES - N_HEADS))).astype(BF16)
    bf = jnp.pad(kv_b_f, (0, LANES - N_HEADS)).reshape(1, LANES)
    kg = jnp.tile(k_norm_g, N_HEADS).reshape(1, d)
    consts = [norm_g.reshape(1, d), wk, wv, wf, bf, kg, gsum, gexp]
    act = jax.ShapeDtypeStruct((rows.rows, d), F32)
    small = jax.ShapeDtypeStruct((rows.rows, N_HEADS), F32)
    return pl.pallas_call(
        functools.partial(_kv_proj_kernel, rows=rows),
        out_shape=(act, act, small, small),
        grid=(rows.n_tiles,),
        in_specs=[rows.row_spec(), sh_spec, sc_spec] + [_full_spec(c) for c in consts],
        out_specs=(rows.row_spec(), rows.row_spec(), rows.row_spec(N_HEADS), rows.row_spec(N_HEADS)),
        scratch_shapes=[pltpu.VMEM((1, LANES), F32)],
        compiler_params=_cparams("arbitrary"),
        name="kv_proj",
    )(x, sh_op, sc_op, *consts)


def _q_proj_kernel(x_ref, sh_ref, sc_ref, g_ref, wq_ref, qg_ref, gsum_ref, gexp_ref, q_ref):
    xn = _modulate(x_ref[...], g_ref[...], sh_ref[...], sc_ref[...])
    q = _head_rms(_bdot(xn, wq_ref[...]), gsum_ref[...], gexp_ref[...], qg_ref[...])
    q_ref[...] = q * (HEAD_DIM ** -0.5)


def q_proj(rows, x, sh, sc, norm_g, wq, q_norm_g):
    d = D_MODEL
    sh_op, sh_spec = rows.mod_operand(sh)
    sc_op, sc_spec = rows.mod_operand(sc)
    gsum, gexp = _group_mats()
    consts = [norm_g.reshape(1, d), wq.astype(BF16), jnp.tile(q_norm_g, N_HEADS).reshape(1, d), gsum, gexp]
    return pl.pallas_call(
        _q_proj_kernel,
        out_shape=jax.ShapeDtypeStruct((rows.rows, d), F32),
        grid=(rows.n_tiles,),
        in_specs=[rows.row_spec(), sh_spec, sc_spec] + [_full_spec(c) for c in consts],
        out_specs=rows.row_spec(),
        compiler_params=_cparams("parallel"),
        name="q_proj",
    )(x, sh_op, sc_op, *consts)


def _fox_prompt_kernel(q_ref, k_ref, v_ref, f_ref, o_ref, m_s, l_s, acc_s, *, tile):
    hp = pl.program_id(1)
    qi = pl.program_id(2)
    lane = lax.broadcasted_iota(jnp.int32, (1, LANES), 1)
    q = q_ref[...]
    qms = []
    for hh in range(2):
        qms.append(jnp.where((lane // HEAD_DIM) == hh, q, 0.0).astype(BF16))
        m_s[hh] = jnp.full((1, tile), NEG_INF, F32)
        l_s[hh] = jnp.zeros((1, tile), F32)
        acc_s[hh] = jnp.zeros((LANES, tile), F32)

    def step(ki, diagonal):
        start = pl.multiple_of(ki * tile, tile)
        kt = k_ref[pl.ds(start, tile), :].astype(BF16)
        vt = v_ref[pl.ds(start, tile), :].astype(BF16)
        fk = f_ref[pl.ds(start, tile), :]
        lane_h = lax.broadcasted_iota(jnp.int32, fk.shape, 1)
        for hh in range(2):
            f_key = jnp.sum(jnp.where(lane_h == 2 * hp + hh, fk, 0.0), axis=-1, keepdims=True)
            s = lax.dot_general(kt, qms[hh], (((1,), (1,)), ((), ())), preferred_element_type=F32) - f_key
            if diagonal:
                key = lax.broadcasted_iota(jnp.int32, (tile, tile), 0)
                qry = lax.broadcasted_iota(jnp.int32, (tile, tile), 1)
                s = jnp.where(key <= qry, s, NEG_INF)
            m_old = m_s[hh]
            m_new = jnp.maximum(m_old, jnp.max(s, axis=0, keepdims=True))
            p = jnp.exp(s - m_new)
            alpha = jnp.exp(m_old - m_new)
            l_s[hh] = alpha * l_s[hh] + jnp.sum(p, axis=0, keepdims=True)
            acc_s[hh] = alpha * acc_s[hh] + lax.dot_general(
                vt, p.astype(BF16), (((0,), (0,)), ((), ())), preferred_element_type=F32)
            m_s[hh] = m_new

    def body(ki, c):
        step(ki, False)
        return c

    lax.fori_loop(0, qi, body, 0)
    step(qi, True)
    chan = lax.broadcasted_iota(jnp.int32, (LANES, 1), 0)
    o_t = jnp.where((chan // HEAD_DIM) == 0, acc_s[0] / l_s[0], acc_s[1] / l_s[1])
    o_ref[...] = o_t.T


def fox_prompt_attention(q, k, v, fcum, bsz, t_len, tile=512):
    d = D_MODEL
    tile = min(tile, t_len)
    q3, k3, v3 = (z.reshape(bsz, t_len, d) for z in (q, k, v))
    f3 = fcum.reshape(bsz, t_len, N_HEADS)
    out = pl.pallas_call(
        functools.partial(_fox_prompt_kernel, tile=tile),
        out_shape=jax.ShapeDtypeStruct((bsz, t_len, d), F32),
        grid=(bsz, N_HEADS // 2, t_len // tile),
        in_specs=[pl.BlockSpec((None, tile, LANES), lambda b, h, i: (b, i, h)),
                  pl.BlockSpec((None, t_len, LANES), lambda b, h, i: (b, 0, h)),
                  pl.BlockSpec((None, t_len, LANES), lambda b, h, i: (b, 0, h)),
                  pl.BlockSpec((None, t_len, N_HEADS), lambda b, h, i: (b, 0, 0))],
        out_specs=pl.BlockSpec((None, tile, LANES), lambda b, h, i: (b, i, h)),
        scratch_shapes=[pltpu.VMEM((2, 1, tile), F32), pltpu.VMEM((2, 1, tile), F32),
                        pltpu.VMEM((2, LANES, tile), F32)],
        compiler_params=_cparams("parallel", "parallel", "arbitrary"),
        name="fox_prompt",
    )(q3, k3, v3, f3)
    return out.reshape(bsz * t_len, d)


def _fox_sample_kernel(pt_ref, q_ref, kn_ref, vn_ref, fnr_ref, *rest, t_len, page, pps):
    page_refs = rest[:3 * pps]
    o_ref, qbd_s, m_s, l_s, acc_s, rc_s = rest[3 * pps:]
    p = pl.program_id(1)
    nrow = t_len * N_HEADS
    head_of_row = lax.broadcasted_iota(jnp.int32, (nrow, D_MODEL), 0) % N_HEADS
    head_of_col = lax.broadcasted_iota(jnp.int32, (nrow, D_MODEL), 1) // HEAD_DIM
    diag = head_of_row == head_of_col

    @pl.when(p == 0)
    def _():
        q = q_ref[...]
        qrep = jnp.broadcast_to(q[:, None, :], (t_len, N_HEADS, D_MODEL)).reshape(nrow, D_MODEL)
        qbd = jnp.where(diag, qrep, 0.0)
        qbd_s[...] = qbd.astype(BF16)
        q_of_row = lax.broadcasted_iota(jnp.int32, (nrow, 1), 0) // N_HEADS
        fnr = fnr_ref[...]
        s_cols = []
        for t in range(t_len):
            s_t = jnp.sum(qbd * kn_ref[pl.ds(t, 1), :], axis=-1, keepdims=True)
            s_cols.append(jnp.where(q_of_row >= t, s_t - fnr[:, t:t + 1], NEG_INF))
        m = functools.reduce(jnp.maximum, s_cols)
        l = jnp.zeros((nrow, 1), F32)
        acc = jnp.zeros((nrow, D_MODEL), F32)
        for t in range(t_len):
            p_t = jnp.exp(s_cols[t] - m)
            l = l + p_t
            acc = acc + p_t * vn_ref[pl.ds(t, 1), :]
        m_s[...] = m
        l_s[...] = l
        acc_s[...] = acc
        rc_s[...] = jnp.zeros_like(rc_s)

    ji = lax.broadcasted_iota(jnp.int32, (page, page), 0)
    si = lax.broadcasted_iota(jnp.int32, (page, page), 1)
    later = (ji > si).astype(BF16)
    qbd = qbd_s[...]
    rc = rc_s[...]
    s_parts = []
    for i in range(pps):
        k_ref, _, lf_ref = page_refs[3 * i:3 * i + 3]
        lf = lf_ref[...]
        r16 = _split_dot(lf, later) + rc
        rc = rc + jnp.sum(lf, axis=-1, keepdims=True)
        kt = k_ref[...].reshape(D_MODEL, page).astype(BF16)
        s = jnp.dot(qbd, kt, preferred_element_type=F32)
        s_parts.append(s + jnp.concatenate([r16] * t_len, axis=0))
    rc_s[...] = rc
    m_old = m_s[...]
    m_new = functools.reduce(jnp.maximum, [m_old] + [jnp.max(s, axis=-1, keepdims=True) for s in s_parts])
    alpha = jnp.exp(m_old - m_new)
    l = alpha * l_s[...]
    acc = alpha * acc_s[...]
    for i, s in enumerate(s_parts):
        pr = jnp.exp(s - m_new)
        l = l + jnp.sum(pr, axis=-1, keepdims=True)
        vt = page_refs[3 * i + 1][...].reshape(D_MODEL, page).astype(BF16)
        acc = acc + lax.dot_general(pr.astype(BF16), vt, (((1,), (1,)), ((), ())),
                                    preferred_element_type=F32)
    l_s[...] = l
    acc_s[...] = acc
    m_s[...] = m_new

    @pl.when(p == pl.num_programs(1) - 1)
    def _():
        o = jnp.where(diag, acc_s[...] / l_s[...], 0.0)
        o_ref[...] = jnp.sum(o.reshape(t_len, N_HEADS, D_MODEL), axis=1)


def fox_sample_attention(q, k_new, v_new, fcum, cache_k, cache_v, cache_logf, page_table, bsz, t_len, pps=8):
    d = D_MODEL
    n_pool, page = cache_k.shape[:2]
    n_pages = page_table.shape[1]
    pps = min(pps, n_pages)
    assert n_pages % pps == 0
    nrow = t_len * N_HEADS
    q3, kn3, vn3 = (z.reshape(bsz, t_len, d) for z in (q, k_new, v_new))
    fc3 = fcum.reshape(bsz, t_len, N_HEADS)
    fnr = jnp.broadcast_to(jnp.transpose(fc3, (0, 2, 1))[:, None], (bsz, t_len, N_HEADS, t_len)).reshape(
        bsz, nrow, t_len)
    ck = jnp.transpose(cache_k, (0, 2, 3, 1))
    cv = jnp.transpose(cache_v, (0, 2, 3, 1))
    clf = jnp.transpose(cache_logf, (0, 2, 1))
    per_b = lambda shape: pl.BlockSpec((None,) + shape, lambda b, p, pt: (b,) + (0,) * len(shape))

    def paged(shape, i):
        return pl.BlockSpec((None,) + shape,
                            lambda b, p, pt: (pt[b, n_pages - 1 - (p * pps + i)],) + (0,) * len(shape))

    page_specs, page_ops = [], []
    for i in range(pps):
        page_specs += [paged((N_HEADS, HEAD_DIM, page), i), paged((N_HEADS, HEAD_DIM, page), i),
                       paged((N_HEADS, page), i)]
        page_ops += [ck, cv, clf]
    out = pl.pallas_call(
        functools.partial(_fox_sample_kernel, t_len=t_len, page=page, pps=pps),
        out_shape=jax.ShapeDtypeStruct((bsz, t_len, d), F32),
        grid_spec=pltpu.PrefetchScalarGridSpec(
            num_scalar_prefetch=1,
            grid=(bsz, n_pages // pps),
            in_specs=[per_b((t_len, d)), per_b((t_len, d)), per_b((t_len, d)), per_b((nrow, t_len))]
            + page_specs,
            out_specs=per_b((t_len, d)),
            scratch_shapes=[pltpu.VMEM((nrow, d), BF16), pltpu.VMEM((nrow, 1), F32),
                            pltpu.VMEM((nrow, 1), F32), pltpu.VMEM((nrow, d), F32),
                            pltpu.VMEM((N_HEADS, 1), F32)]),
        compiler_params=_cparams("parallel", "arbitrary"),
        name="fox_sample",
    )(page_table, q3, kn3, vn3, fnr, *page_ops)
    return out.reshape(bsz * t_len, d)


def _trunk(x3, mods, kv_mods, shift0, wkv0, attend, p, moe_w, moe_w32, depth=DEPTH):
    bsz, t_len, d = x3.shape
    x = x3.reshape(bsz * t_len, d)
    shifts, states = [], []
    v_first = None
    k_sh = v_sh = lf_sh = fcum = None
    for i in range(depth):
        precise = i < N_A and not _Rows(bsz, t_len, 256).long
        rows = _Rows(bsz, t_len, 256, precise)
        rows_moe = _Rows(bsz, t_len, 512, precise)
        ew = moe_w32 if precise else moe_w
        sh_m, sc_m, gt_m, sh_f, sc_f, gt_f = jnp.split(mods[i], 6, axis=-1)
        if i < N_A:
            in_rows = rows.long and bsz % LANE_BATCH == 0
            order = jnp.asarray(CHANNEL_ORDER, jnp.int32)
            cols = (lambda z: jnp.take(z, order, axis=-1)) if in_rows else (lambda z: z)
            vmix = None if i == 0 else (v_first, cols(p['a_v0'][i - 1]), p['a_v1'][i - 1], cols(p['a_v2'][i - 1]))
            r, w, k, v, a, g, last = rwkv_proj(
                rows, x, sh_m, sc_m, shift0[i], p['norm_g'][i, 0], p['a_mu'][i], cols(p['a_w0'][i]),
                cols(p['a_a0'][i]), cols(p['a_wr'][i]), cols(p['a_wk'][i]), cols(p['a_wv'][i]), p['a_w1'][i],
                cols(p['a_w2'][i]), p['a_a1'][i], cols(p['a_a2'][i]), p['a_g1'][i], cols(p['a_g2'][i]), vmix)
            if i == 0:
                v_first = v
            shifts.append(last)
            pars = [_param_lanes(z, bsz) for z in (p['a_kk'][i], p['a_ka'][i], p['a_rk'][i].reshape(-1),
                                                   p['a_lnx_w'][i], p['a_lnx_b'][i])]
            s0 = jnp.transpose(wkv0[i], (3, 2, 0, 1)).reshape(HEAD_DIM, HEAD_DIM, bsz * N_HEADS)
            if in_rows:
                y, s_t = wkv_scan_rows(*[z.reshape(bsz, t_len, d) for z in (r, w, k, v, a)], *pars, s0)
                x = resid_mm(rows, y.reshape(bsz * t_len, d), g, jnp.take(p['a_wo'][i], order, axis=0), x, gt_m)
            else:
                y, s_t = wkv_scan(*[_to_lanes(z, bsz, t_len) for z in (r, w, k, v, a)], *pars, s0)
                x = resid_mm(rows, _from_lanes(y, bsz, t_len), g, p['a_wo'][i], x, gt_m)
            states.append(jnp.transpose(s_t.reshape(HEAD_DIM, HEAD_DIM, bsz, N_HEADS), (2, 3, 1, 0)))
        else:
            j = i - N_A
            if j == 0:
                k_sh, v_sh, lf_sh, fcum = kv_proj(rows, x, kv_mods[0], kv_mods[1], p['kv_norm_g'], p['kv_w'],
                                                  p['kv_b_f'], p['k_norm_g'])
            q = q_proj(rows, x, sh_m, sc_m, p['norm_g'][i, 0], p['b_wq'][j], p['b_q_norm_g'][j])
            o = attend(q, k_sh, v_sh, fcum)
            x = resid_mm(rows, o, None, p['b_wo'][j], x, gt_m)
        if rows_moe.rows >= SPARSE_MOE_MIN_ROWS:
            xg, grp = moe_route(rows_moe, x, sh_f, sc_f, p['norm_g'][i, 1], p['router_w'], p['router_bias'])
            y = moe_sparse(xg, grp.reshape(-1), ew[0], ew[1], ew[2], i)
            x = resid_add(rows_moe, x, y, gt_f)
        else:
            x = moe_layer(rows_moe, x, sh_f, sc_f, gt_f, p['norm_g'][i, 1], p['router_w'], p['router_bias'],
                          ew[0], ew[1], ew[2], i)
    if depth < DEPTH:
        return x.reshape(bsz, t_len, d), jnp.stack(shifts)
    hd = (bsz, t_len, N_HEADS, HEAD_DIM)
    return (x.reshape(bsz, t_len, d), jnp.stack(shifts), jnp.stack(states), k_sh.reshape(hd),
            v_sh.reshape(hd), lf_sh.reshape(bsz, t_len, N_HEADS))


def kernel(x_prompt, x_sample, c_prompt, c_sample, state_shift, state_wkv, cache_k, cache_v, cache_logf, page_table, mod_w, mod_b, norm_g, a_mu, a_w0, a_w1, a_w2, a_a0, a_a1, a_a2, a_v0, a_v1, a_v2, a_g1, a_g2, a_kk, a_ka, a_rk, a_wr, a_wk, a_wv, a_wo, a_lnx_w, a_lnx_b, kv_mod_w, kv_mod_b, kv_norm_g, kv_w, kv_b_f, k_norm_g, b_wq, b_q_norm_g, b_wo, router_w, router_bias, moe_w_gate, moe_w_up, moe_w_down):
    p = dict(norm_g=norm_g, a_mu=a_mu, a_w0=a_w0, a_w1=a_w1, a_w2=a_w2, a_a0=a_a0, a_a1=a_a1, a_a2=a_a2,
             a_v0=a_v0, a_v1=a_v1, a_v2=a_v2, a_g1=a_g1, a_g2=a_g2, a_kk=a_kk, a_ka=a_ka, a_rk=a_rk,
             a_wr=a_wr, a_wk=a_wk, a_wv=a_wv, a_wo=a_wo, a_lnx_w=a_lnx_w, a_lnx_b=a_lnx_b,
             kv_norm_g=kv_norm_g, kv_w=kv_w, kv_b_f=kv_b_f, k_norm_g=k_norm_g, b_wq=b_wq,
             b_q_norm_g=b_q_norm_g, b_wo=b_wo, router_w=router_w, router_bias=router_bias)
    d = D_MODEL
    bp, t_p = x_prompt.shape[:2]
    bs, t_s = x_sample.shape[:2]
    moe_w = (moe_w_gate.astype(BF16), moe_w_up.astype(BF16), moe_w_down.astype(BF16))

    w_all = jnp.concatenate([mod_w[i] for i in range(DEPTH)] + [kv_mod_w], axis=1)
    b_all = jnp.concatenate([mod_b[i] for i in range(DEPTH)] + [kv_mod_b], axis=0)
    m_all = adaln_all(jnp.concatenate([c_prompt, c_sample], axis=0), w_all, b_all)

    def split_mods(m):
        layer = [m[:, i * 6 * d:(i + 1) * 6 * d] for i in range(DEPTH)]
        kvm = m[:, DEPTH * 6 * d:]
        return layer, (kvm[:, :d], kvm[:, d:])

    mods_p, kvm_p = split_mods(m_all[:bp])
    mods_s, kvm_s = split_mods(m_all[bp:])

    def attend_prompt(q, k, v, fcum):
        return fox_prompt_attention(q, k, v, fcum, bp, t_p)

    def attend_sample(q, k, v, fcum):
        return fox_sample_attention(q, k, v, fcum, cache_k, cache_v, cache_logf, page_table, bs, t_s)

    shift0_p = jnp.zeros((N_A, bp, d), F32)
    wkv0_p = jnp.zeros((N_A, bp, N_HEADS, HEAD_DIM, HEAD_DIM), F32)
    moe_w32 = (moe_w_gate, moe_w_up, moe_w_down)
    y_p, shift_p, wkv_p, k_p, v_p, lf_p = _trunk(x_prompt, mods_p, kvm_p, shift0_p, wkv0_p, attend_prompt,
                                                 p, moe_w, moe_w32)
    y_s, shift_s, wkv_s, k_s, v_s, lf_s = _trunk(x_sample, mods_s, kvm_s, state_shift, state_wkv,
                                                 attend_sample, p, moe_w, moe_w32)
    return (y_p, y_s, shift_p, wkv_p, k_p, v_p, lf_p, shift_s, wkv_s, k_s, v_s, lf_s)
```

```python
import functools

import jax
import jax.numpy as jnp
from jax import lax
from jax.experimental import pallas as pl
from jax.experimental.pallas import tpu as pltpu

D_MODEL = 1024
HEAD_DIM = 64
N_HEADS = D_MODEL // HEAD_DIM
N_EXPERTS = 16
N_GROUPS = 4
EXPERTS_PER_GROUP = N_EXPERTS // N_GROUPS
D_EXPERT = 512
DEPTH = 4
N_A = 2
RMS_EPS = 1e-6
GN_EPS = 64e-5
LANES = 128
VMEM_LIMIT = 56 * 1024 * 1024
SPARSE_MOE_MIN_ROWS = 4096

F32 = jnp.float32
BF16 = jnp.bfloat16
NEG_INF = float("-inf")
LOG2E = 1.4426950408889634


def _cparams(*sem):
    return pltpu.CompilerParams(dimension_semantics=sem, vmem_limit_bytes=VMEM_LIMIT)


def _bdot(a, b):
    return jnp.dot(a.astype(BF16), b.astype(BF16), preferred_element_type=F32)


def _dot3(a, b):
    a = a.astype(F32)
    b = b.astype(F32)
    a1 = a.astype(BF16)
    a2 = (a - a1.astype(F32)).astype(BF16)
    b1 = b.astype(BF16)
    b2 = (b - b1.astype(F32)).astype(BF16)
    return (jnp.dot(a1, b1, preferred_element_type=F32)
            + (jnp.dot(a1, b2, preferred_element_type=F32) + jnp.dot(a2, b1, preferred_element_type=F32)))


def _mm(a, b, precise):
    return _dot3(a, b) if precise else _bdot(a, b)


def _split_dot(a, m):
    a1 = a.astype(BF16)
    r1 = a - a1.astype(F32)
    a2 = r1.astype(BF16)
    a3 = (r1 - a2.astype(F32)).astype(BF16)
    return (jnp.dot(a1, m, preferred_element_type=F32) + jnp.dot(a2, m, preferred_element_type=F32)
            + jnp.dot(a3, m, preferred_element_type=F32))


def _sigmoid(z):
    return 1.0 / (1.0 + jnp.exp(-z))


def _silu(z):
    return z * _sigmoid(z)


def _softplus(z):
    return jnp.maximum(z, 0.0) + jnp.log(1.0 + jnp.exp(-jnp.abs(z)))


def _modulate(x, g, sh, sc):
    y = x * lax.rsqrt(jnp.mean(x * x, axis=-1, keepdims=True) + RMS_EPS)
    return (y * g) * (1.0 + sc) + sh


def _head_rms(z, gsum, gexp, g_row):
    ss = _split_dot(z * z, gsum)
    inv = lax.rsqrt(ss * (1.0 / HEAD_DIM) + RMS_EPS)
    return z * _split_dot(inv, gexp) * g_row


def _group_mats():
    head_of_lane = jnp.arange(D_MODEL) // HEAD_DIM
    gsum = (head_of_lane[:, None] == jnp.arange(LANES)[None, :]).astype(BF16)
    return gsum, gsum.T


class _Rows:
    def __init__(self, bsz, t_len, tile, precise=False):
        self.bsz, self.t_len = bsz, t_len
        self.precise = precise
        self.rows = bsz * t_len
        self.tm = min(tile, self.rows)
        self.long = t_len >= self.tm
        if self.long:
            assert t_len % self.tm == 0
        else:
            assert self.tm % t_len == 0
        assert self.rows % self.tm == 0
        self.n_tiles = self.rows // self.tm
        self.tiles_per_seq = t_len // self.tm if self.long else 1

    def row_spec(self, width=D_MODEL):
        return pl.BlockSpec((self.tm, width), lambda i, *_: (i, 0))

    def mod_operand(self, m):
        w = m.shape[-1]
        if self.long:
            tps = self.tiles_per_seq
            return m.reshape(self.bsz, 1, w), pl.BlockSpec((None, 1, w), lambda i, *_: (i // tps, 0, 0))
        return jnp.repeat(m, self.t_len, axis=0), pl.BlockSpec((self.tm, w), lambda i, *_: (i, 0))


def _full_spec(a):
    nd = a.ndim
    return pl.BlockSpec(a.shape, lambda *_: (0,) * nd)


def _adaln_kernel(c_ref, w_ref, b_ref, o_ref):
    c = c_ref[...]
    o_ref[...] = jnp.dot(_silu(c), w_ref[...], precision=lax.Precision.HIGHEST,
                         preferred_element_type=F32) + b_ref[...]


def adaln_all(c, w, b, tn=1024):
    m, k = c.shape
    n = w.shape[1]
    return pl.pallas_call(
        _adaln_kernel,
        out_shape=jax.ShapeDtypeStruct((m, n), F32),
        grid=(n // tn,),
        in_specs=[pl.BlockSpec((m, k), lambda j: (0, 0)), pl.BlockSpec((k, tn), lambda j: (0, j)),
                  pl.BlockSpec((1, tn), lambda j: (0, j))],
        out_specs=pl.BlockSpec((m, tn), lambda j: (0, j)),
        compiler_params=_cparams("parallel"),
        name="adaln",
    )(c, w, b.reshape(1, n))


def _rwkv_proj_kernel(*refs, rows, has_vmix):
    (x_ref, sh_ref, sc_ref, first_ref, g_ref, mu_ref, w0_ref, a0_ref,
     wr_ref, wk_ref, wv_ref, w1_ref, w2_ref, a1_ref, a2_ref, g1_ref, g2_ref) = refs[:17]
    pos = 17
    if has_vmix:
        vf_ref, v0_ref, v1_ref, v2_ref = refs[pos:pos + 4]
        pos += 4
    r_ref, w_ref, k_ref, v_ref, a_ref, gg_ref, last_ref = refs[pos:pos + 7]
    carry = refs[pos + 7]
    tm = rows.tm
    i = pl.program_id(0)

    xn = _modulate(x_ref[...], g_ref[...], sh_ref[...], sc_ref[...])
    rolled = pltpu.roll(xn, 1, 0)
    row = lax.broadcasted_iota(jnp.int32, (tm, 1), 0)
    if rows.long:
        @pl.when(i % rows.tiles_per_seq == 0)
        def _():
            carry[...] = first_ref[...]
        x_prev = jnp.where(row == 0, carry[...], rolled)
        carry[...] = xn[tm - 1:tm, :]
        last_ref[...] = xn[tm - 1:tm, :]
    else:
        x_prev = jnp.where(row % rows.t_len == 0, first_ref[...], rolled)
        last_ref[...] = xn
    xx = x_prev - xn
    mu = mu_ref[...]

    mm = functools.partial(_mm, precise=rows.precise)

    def mix(j):
        z = xn + xx * mu[j:j + 1, :]
        return z if rows.precise else z.astype(BF16)

    xv = mix(3)
    r_ref[...] = mm(mix(0), wr_ref[...])
    w_ref[...] = -_softplus(-(w0_ref[...] + mm(jnp.tanh(mm(mix(1), w1_ref[...])), w2_ref[...]))) - 0.5
    k_ref[...] = mm(mix(2), wk_ref[...])
    v = mm(xv, wv_ref[...])
    if has_vmix:
        v = v + (vf_ref[...] - v) * _sigmoid(v0_ref[...] + mm(mm(xv, v1_ref[...]), v2_ref[...]))
    v_ref[...] = v
    a_ref[...] = _sigmoid(a0_ref[...] + mm(mm(mix(4), a1_ref[...]), a2_ref[...]))
    gg_ref[...] = mm(_sigmoid(mm(mix(5), g1_ref[...])), g2_ref[...])


def rwkv_proj(rows, x, sh, sc, shift0, norm_g, mu, w0, a0, wr, wk, wv, w1, w2, a1, a2, g1, g2, vmix):
    d = D_MODEL
    sh_op, sh_spec = rows.mod_operand(sh)
    sc_op, sc_spec = rows.mod_operand(sc)
    first_op, first_spec = rows.mod_operand(shift0)
    row2 = lambda z: z.reshape(1, -1)
    wcast = (lambda z: z) if rows.precise else (lambda z: z.astype(BF16))
    ops = [x, sh_op, sc_op, first_op, row2(norm_g), mu, row2(w0), row2(a0)] + [
        wcast(z) for z in (wr, wk, wv, w1, w2, a1, a2, g1, g2)]
    specs = [rows.row_spec(), sh_spec, sc_spec, first_spec] + [_full_spec(o) for o in ops[4:]]
    if vmix is not None:
        v_first, v0, v1, v2 = vmix
        extra = [v_first, row2(v0), wcast(v1), wcast(v2)]
        ops += extra
        specs += [rows.row_spec()] + [_full_spec(o) for o in extra[1:]]
    act = jax.ShapeDtypeStruct((rows.rows, d), F32)
    if rows.long:
        tps = rows.tiles_per_seq
        last_shape = jax.ShapeDtypeStruct((rows.bsz, 1, d), F32)
        last_spec = pl.BlockSpec((None, 1, d), lambda i: (i // tps, 0, 0))
    else:
        last_shape, last_spec = act, rows.row_spec()
    outs = pl.pallas_call(
        functools.partial(_rwkv_proj_kernel, rows=rows, has_vmix=vmix is not None),
        out_shape=(act,) * 6 + (last_shape,),
        grid=(rows.n_tiles,),
        in_specs=specs,
        out_specs=(rows.row_spec(),) * 6 + (last_spec,),
        scratch_shapes=[pltpu.VMEM((1, d), F32)],
        compiler_params=_cparams("arbitrary"),
        name="rwkv_proj",
    )(*ops)
    last = outs[6]
    last = last.reshape(rows.bsz, d) if rows.long else last.reshape(rows.bsz, rows.t_len, d)[:, -1]
    return outs[:6] + (last,)


def _wkv_kernel(r_ref, w_ref, k_ref, v_ref, a_ref, kk_ref, ka_ref, rk_ref, lw_ref, lb_ref, s0_ref,
                y_ref, st_ref, S, nk_s, bb_s, km_s, dc_s, y_s, *, tt):
    tb = pl.program_id(1)

    @pl.when(tb == 0)
    def _():
        S[...] = s0_ref[...]

    k = k_ref[...]
    a = a_ref[...]
    kkr = k * kk_ref[...][None]
    ss = jnp.sum(kkr * kkr, axis=1, keepdims=True)
    kk = kkr * lax.rsqrt(jnp.maximum(ss, 1e-24))
    km = k * (1.0 + (a - 1.0) * ka_ref[...][None])
    nk_s[...] = -kk
    bb_s[...] = kk * a
    km_s[...] = km
    dc_s[...] = jnp.exp(-jnp.exp(w_ref[...]))

    def step(t, carry):
        def row(ref, j):
            return jnp.broadcast_to(ref[t, pl.ds(j, 1), :], (HEAD_DIM, LANES))

        sa = S[0] * row(nk_s, 0)
        for j in range(1, HEAD_DIM):
            sa = sa + S[j] * row(nk_s, j)
        vt = v_ref[t]
        y = None
        for j in range(HEAD_DIM):
            s_new = S[j] * row(dc_s, j) + sa * row(bb_s, j) + vt * row(km_s, j)
            S[j] = s_new
            yj = s_new * row(r_ref, j)
            y = yj if y is None else y + yj
        y_s[t] = y
        return carry

    lax.fori_loop(0, tt, step, 0)

    y = y_s[...]
    mean = jnp.mean(y, axis=1, keepdims=True)
    yc = y - mean
    var = jnp.mean(yc * yc, axis=1, keepdims=True)
    bonus = jnp.sum(r_ref[...] * km * rk_ref[...][None], axis=1, keepdims=True) * v_ref[...]
    y_ref[...] = yc * lax.rsqrt(var + GN_EPS) * lw_ref[...][None] + lb_ref[...][None] + bonus

    @pl.when(tb == pl.num_programs(1) - 1)
    def _():
        st_ref[...] = S[...]


def wkv_scan(r, w, k, v, a, kk_p, ka_p, rk_p, lw_p, lb_p, s0, *, tt=16):
    t_len, n, l_tot = r.shape
    tt = min(tt, t_len)
    assert t_len % tt == 0 and l_tot % LANES == 0
    seq_spec = pl.BlockSpec((tt, n, LANES), lambda g, t: (t, 0, g))
    par_spec = pl.BlockSpec((n, LANES), lambda g, t: (0, g))
    st_spec = pl.BlockSpec((n, n, LANES), lambda g, t: (0, 0, g))
    return pl.pallas_call(
        functools.partial(_wkv_kernel, tt=tt),
        out_shape=(jax.ShapeDtypeStruct(r.shape, F32), jax.ShapeDtypeStruct(s0.shape, F32)),
        grid=(l_tot // LANES, t_len // tt),
        in_specs=[seq_spec] * 5 + [par_spec] * 5 + [st_spec],
        out_specs=(seq_spec, st_spec),
        scratch_shapes=[pltpu.VMEM((n, n, LANES), F32)] + [pltpu.VMEM((tt, n, LANES), F32)] * 5,
        compiler_params=_cparams("parallel", "arbitrary"),
        name="wkv_scan",
    )(r, w, k, v, a, kk_p, ka_p, rk_p, lw_p, lb_p, s0)


LANE_BATCH = LANES // N_HEADS
CHANNEL_ORDER = tuple(int(h * HEAD_DIM + n) for n in range(HEAD_DIM) for h in range(N_HEADS))


def _wkv_rows_kernel(r_ref, w_ref, k_ref, v_ref, a_ref, kk_ref, ka_ref, rk_ref, lw_ref, lb_ref, s0_ref,
                     y_ref, st_ref, S, r_s, dc_s, km_s, v_s, bb_s, nk_s, y_s, *, tt):
    tb = pl.program_id(1)
    n = HEAD_DIM
    n_chunks = D_MODEL // LANES
    per_chunk = LANES // N_HEADS

    @pl.when(tb == 0)
    def _():
        S[...] = s0_ref[...]

    def to_lanes(src, dst, time_major):
        for c in range(n_chunks):
            xt = [src[b, :, c * LANES:(c + 1) * LANES].T for b in range(LANE_BATCH)]
            for i in range(per_chunk):
                m = jnp.concatenate([x[i * N_HEADS:(i + 1) * N_HEADS, :] for x in xt], axis=0)
                ch = c * per_chunk + i
                if time_major:
                    dst[pl.ds(ch, tt, stride=n), :] = m.T
                else:
                    dst[pl.ds(ch * tt, tt), :] = m.T

    def from_lanes(src, dst):
        for c in range(n_chunks):
            parts = [src[pl.ds(c * per_chunk + i, tt, stride=n), :].T for i in range(per_chunk)]
            for b in range(LANE_BATCH):
                xb = jnp.concatenate([p[b * N_HEADS:(b + 1) * N_HEADS, :] for p in parts], axis=0)
                dst[b, :, c * LANES:(c + 1) * LANES] = xb.T

    to_lanes(r_ref, r_s, False)
    to_lanes(w_ref, dc_s, False)
    to_lanes(k_ref, km_s, False)
    to_lanes(v_ref, v_s, True)
    to_lanes(a_ref, bb_s, False)

    def by_channel(z):
        return z.reshape(n, tt, LANES)

    def per_channel(p_ref):
        return p_ref[...][:, None, :]

    k = by_channel(km_s[...])
    a = by_channel(bb_s[...])
    kkr = k * per_channel(kk_ref)
    ss = jnp.sum(kkr * kkr, axis=0, keepdims=True)
    kk = kkr * lax.rsqrt(jnp.maximum(ss, 1e-24))
    km = k * (1.0 + (a - 1.0) * per_channel(ka_ref))
    rkk = jnp.sum(by_channel(r_s[...]) * km * per_channel(rk_ref), axis=0)
    nk_s[pl.ds(0, tt * n), :] = (-kk).reshape(n * tt, LANES)
    bb_s[...] = (kk * a).reshape(n * tt, LANES)
    km_s[...] = km.reshape(n * tt, LANES)
    dc_s[...] = jnp.exp(-jnp.exp(dc_s[...]))

    def row(ref, t, j):
        return jnp.broadcast_to(ref[pl.ds(j * tt + t, 1), :], (n, LANES))

    nk_s[pl.ds(tt * n, 8), :] = jnp.zeros((8, LANES), F32)
    sa0 = S[0] * row(nk_s, 0, 0)
    for j in range(1, n):
        sa0 = sa0 + S[j] * row(nk_s, 0, j)

    def step(t, sa):
        base = pl.multiple_of(t * n, n)
        vt = v_s[pl.ds(base, n), :]
        y = None
        sa_next = None
        for j in range(n):
            s_new = S[j] * row(dc_s, t, j) + sa * row(bb_s, t, j) + vt * row(km_s, t, j)
            S[j] = s_new
            yj = s_new * row(r_s, t, j)
            y = yj if y is None else y + yj
            sj = s_new * row(nk_s, t + 1, j)
            sa_next = sj if sa_next is None else sa_next + sj
        y_s[pl.ds(base, n), :] = y
        return sa_next

    lax.fori_loop(0, tt, step, sa0)

    y = y_s[...].reshape(tt, n, LANES)
    mean = jnp.mean(y, axis=1, keepdims=True)
    yc = y - mean
    var = jnp.mean(yc * yc, axis=1, keepdims=True)
    bonus = rkk[:, None, :] * v_s[...].reshape(tt, n, LANES)
    y_s[...] = (yc * lax.rsqrt(var + GN_EPS) * lw_ref[...][None] + lb_ref[...][None] + bonus).reshape(
        tt * n, LANES)
    from_lanes(y_s, y_ref)

    @pl.when(tb == pl.num_programs(1) - 1)
    def _():
        st_ref[...] = S[...]


def wkv_scan_rows(r, w, k, v, a, kk_p, ka_p, rk_p, lw_p, lb_p, s0, *, tt=64):
    bsz, t_len, d = r.shape
    n = HEAD_DIM
    assert bsz % LANE_BATCH == 0 and t_len % tt == 0
    seq_spec = pl.BlockSpec((LANE_BATCH, tt, d), lambda g, t: (g, t, 0))
    par_spec = pl.BlockSpec((n, LANES), lambda g, t: (0, g))
    st_spec = pl.BlockSpec((n, n, LANES), lambda g, t: (0, 0, g))
    return pl.pallas_call(
        functools.partial(_wkv_rows_kernel, tt=tt),
        out_shape=(jax.ShapeDtypeStruct(r.shape, F32), jax.ShapeDtypeStruct(s0.shape, F32)),
        grid=(bsz // LANE_BATCH, t_len // tt),
        in_specs=[seq_spec] * 5 + [par_spec] * 5 + [st_spec],
        out_specs=(seq_spec, st_spec),
        scratch_shapes=[pltpu.VMEM((n, n, LANES), F32)] + [pltpu.VMEM((tt * n, LANES), F32)] * 5
        + [pltpu.VMEM(((tt + 1) * n, LANES), F32), pltpu.VMEM((tt * n, LANES), F32)],
        compiler_params=_cparams("parallel", "arbitrary"),
        name="wkv_scan_rows",
    )(r, w, k, v, a, kk_p, ka_p, rk_p, lw_p, lb_p, s0)


def _to_lanes(z, bsz, t_len):
    return jnp.transpose(z.reshape(bsz, t_len, N_HEADS, HEAD_DIM), (1, 3, 0, 2)).reshape(
        t_len, HEAD_DIM, bsz * N_HEADS)


def _from_lanes(z, bsz, t_len):
    return jnp.transpose(z.reshape(t_len, HEAD_DIM, bsz, N_HEADS), (2, 0, 3, 1)).reshape(
        bsz * t_len, D_MODEL)


def _param_lanes(p, bsz):
    return jnp.broadcast_to(p.reshape(1, N_HEADS, HEAD_DIM), (bsz, N_HEADS, HEAD_DIM)).transpose(
        2, 0, 1).reshape(HEAD_DIM, bsz * N_HEADS)


def _resid_mm_kernel(*refs, has_mul, precise):
    if has_mul:
        a_ref, b_ref, w_ref, x_ref, gt_ref, o_ref = refs
        lhs = a_ref[...] * b_ref[...]
    else:
        a_ref, w_ref, x_ref, gt_ref, o_ref = refs
        lhs = a_ref[...]
    o_ref[...] = x_ref[...] + gt_ref[...] * _mm(lhs, w_ref[...], precise)


def resid_mm(rows, a, b, w, x, gate):
    gt_op, gt_spec = rows.mod_operand(gate)
    w = w if rows.precise else w.astype(BF16)
    ops = [a] + ([b] if b is not None else []) + [w, x, gt_op]
    specs = [rows.row_spec()] * (2 if b is not None else 1) + [_full_spec(w), rows.row_spec(), gt_spec]
    return pl.pallas_call(
        functools.partial(_resid_mm_kernel, has_mul=b is not None, precise=rows.precise),
        out_shape=jax.ShapeDtypeStruct(x.shape, F32),
        grid=(rows.n_tiles,),
        in_specs=specs,
        out_specs=rows.row_spec(),
        compiler_params=_cparams("parallel"),
        name="resid_mm",
    )(*ops)


def _route(probs, bias):
    tm = probs.shape[0]
    sel = probs + bias
    lane = lax.broadcasted_iota(jnp.int32, (tm, N_EXPERTS), 1)
    grp = lane // EXPERTS_PER_GROUP

    def top2(mask):
        x = jnp.where(mask, sel, NEG_INF)
        v1 = jnp.max(x, axis=-1, keepdims=True)
        i1 = jnp.min(jnp.where(x == v1, lane, N_EXPERTS), axis=-1, keepdims=True)
        x2 = jnp.where(lane == i1, NEG_INF, x)
        v2 = jnp.max(x2, axis=-1, keepdims=True)
        i2 = jnp.min(jnp.where(x2 == v2, lane, N_EXPERTS), axis=-1, keepdims=True)
        return v1 + v2, i1, i2

    best_score, _, _ = top2(grp == 0)
    best = jnp.zeros((tm, 1), jnp.int32)
    for g in range(1, N_GROUPS):
        score, _, _ = top2(grp == g)
        better = score > best_score
        best = jnp.where(better, g, best)
        best_score = jnp.where(better, score, best_score)
    _, i1, i2 = top2(grp == best)
    chosen = (lane == i1) | (lane == i2)
    w = jnp.where(chosen, probs, 0.0)
    return w / jnp.sum(w, axis=-1, keepdims=True), best


def _router_gate(xn, rw, rb):
    logits = jnp.dot(xn, rw, precision=lax.Precision.HIGHEST, preferred_element_type=F32)
    z = jnp.exp(logits - jnp.max(logits, axis=-1, keepdims=True))
    return _route(z / jnp.sum(z, axis=-1, keepdims=True), rb)


def _moe_kernel(x_ref, sh_ref, sc_ref, gt_ref, g_ref, rw_ref, rb_ref, wg_ref, wu_ref, wd_ref,
                o_ref, xn_s, gate_s, acc_s, *, precise):
    e = pl.program_id(1)

    @pl.when(e == 0)
    def _():
        xn = _modulate(x_ref[...], g_ref[...], sh_ref[...], sc_ref[...])
        xn_s[...] = xn.astype(xn_s.dtype)
        gate_s[...], _ = _router_gate(xn, rw_ref[...], rb_ref[...])
        acc_s[...] = jnp.zeros_like(acc_s)

    acc_s[...] += _expert_ffn(xn_s[...], gate_s[...], e, wg_ref, wu_ref, wd_ref, precise)

    @pl.when(e == pl.num_programs(1) - 1)
    def _():
        o_ref[...] = x_ref[...] + gt_ref[...] * acc_s[...]


def _expert_ffn(xn, gate, expert, wg_ref, wu_ref, wd_ref, precise=False):
    lane = lax.broadcasted_iota(jnp.int32, gate.shape, 1)
    g_e = jnp.sum(jnp.where(lane == expert, gate, 0.0), axis=-1, keepdims=True)
    h = _silu(_mm(xn, wg_ref[...], precise)) * _mm(xn, wu_ref[...], precise)
    return _mm(h * g_e, wd_ref[...], precise)


def moe_layer(rows, x, sh, sc, gate, norm_g, router_w, router_bias, w_gate, w_up, w_down, layer):
    d = D_MODEL
    sh_op, sh_spec = rows.mod_operand(sh)
    sc_op, sc_spec = rows.mod_operand(sc)
    gt_op, gt_spec = rows.mod_operand(gate)
    g2 = norm_g.reshape(1, d)
    rb = router_bias.reshape(1, N_EXPERTS)
    tm = rows.tm
    return pl.pallas_call(
        functools.partial(_moe_kernel, precise=rows.precise),
        out_shape=jax.ShapeDtypeStruct(x.shape, F32),
        grid=(rows.n_tiles, N_EXPERTS),
        in_specs=[rows.row_spec(), sh_spec, sc_spec, gt_spec, _full_spec(g2), _full_spec(router_w),
                  _full_spec(rb),
                  pl.BlockSpec((None, None, d, D_EXPERT), lambda i, e: (layer, e, 0, 0)),
                  pl.BlockSpec((None, None, d, D_EXPERT), lambda i, e: (layer, e, 0, 0)),
                  pl.BlockSpec((None, None, D_EXPERT, d), lambda i, e: (layer, e, 0, 0))],
        out_specs=rows.row_spec(),
        scratch_shapes=[pltpu.VMEM((tm, d), F32 if rows.precise else BF16),
                        pltpu.VMEM((tm, N_EXPERTS), F32), pltpu.VMEM((tm, d), F32)],
        compiler_params=_cparams("parallel", "arbitrary"),
        name="moe",
    )(x, sh_op, sc_op, gt_op, g2, router_w, rb, w_gate, w_up, w_down)


PAYLOAD = D_MODEL + LANES


def _moe_route_kernel(x_ref, sh_ref, sc_ref, g_ref, rw_ref, rb_ref, xg_ref, grp_ref):
    xn = _modulate(x_ref[...], g_ref[...], sh_ref[...], sc_ref[...])
    gate, best = _router_gate(xn, rw_ref[...], rb_ref[...])
    xg_ref[:, :D_MODEL] = xn
    xg_ref[:, D_MODEL:] = jnp.zeros((xn.shape[0], LANES), F32)
    xg_ref[:, D_MODEL:D_MODEL + N_EXPERTS] = gate
    grp_ref[...] = best


def moe_route(rows, x, sh, sc, norm_g, router_w, router_bias):
    d = D_MODEL
    sh_op, sh_spec = rows.mod_operand(sh)
    sc_op, sc_spec = rows.mod_operand(sc)
    g2 = norm_g.reshape(1, d)
    rb = router_bias.reshape(1, N_EXPERTS)
    return pl.pallas_call(
        _moe_route_kernel,
        out_shape=(jax.ShapeDtypeStruct((rows.rows, PAYLOAD), F32),
                   jax.ShapeDtypeStruct((rows.rows, 1), jnp.int32)),
        grid=(rows.n_tiles,),
        in_specs=[rows.row_spec(), sh_spec, sc_spec, _full_spec(g2), _full_spec(router_w), _full_spec(rb)],
        out_specs=(rows.row_spec(PAYLOAD), rows.row_spec(1)),
        compiler_params=_cparams("parallel"),
        name="moe_route",
    )(x, sh_op, sc_op, g2, router_w, rb)


def _group_order(grp, tm):
    n = grp.shape[0]
    p_slots = n + N_GROUPS * tm
    onehot = (grp[:, None] == jnp.arange(N_GROUPS, dtype=jnp.int32)[None, :]).astype(jnp.int32)
    counts = jnp.sum(onehot, axis=0)
    rank = jnp.sum((jnp.cumsum(onehot, axis=0) - onehot) * onehot, axis=1)
    padded = ((counts + tm - 1) // tm) * tm
    ends = jnp.cumsum(padded)
    pos = jnp.sum(onehot * (ends - padded)[None, :], axis=1) + rank
    token = jnp.arange(n, dtype=jnp.int32)
    filled = jnp.zeros((p_slots + 2 * tm,), jnp.int32).at[pos].set(token + 1)
    src = jnp.maximum(filled - 1, 0)
    real = jnp.minimum(filled[:p_slots], 1)
    dst = jnp.where(real == 1, src[:p_slots], n + jnp.cumsum(1 - real) - 1).astype(jnp.int32)
    dst_prev = jnp.concatenate([p_slots + jnp.arange(tm, dtype=jnp.int32), dst])
    tile_start = jnp.arange(p_slots // tm + 1, dtype=jnp.int32) * tm
    tile_grp = jnp.sum((tile_start[:, None] >= ends[None, :]).astype(jnp.int32), axis=1)
    tile_grp = jnp.where(tile_grp < N_GROUPS, tile_grp, -1).astype(jnp.int32)
    return src, dst_prev, tile_grp


def _moe_sparse_kernel(src_ref, dst_ref, tg_ref, xg_hbm, wg_ref, wu_ref, wd_ref, y_hbm,
                       xbuf, ybuf, xn_s, gate_s, acc_s, gsem, ssem, *, tm):
    i = pl.program_id(0)
    e = pl.program_id(1)
    nt = pl.num_programs(0)
    slot = i % 2
    prev = 1 - slot
    quarter = tm // EXPERTS_PER_GROUP

    def row_in(tile, to_slot, k):
        return pltpu.make_async_copy(xg_hbm.at[pl.ds(src_ref[tile * tm + k], 1), :],
                                     xbuf.at[to_slot, pl.ds(k, 1), :], gsem.at[to_slot])

    def row_out(tile, from_slot, k):
        return pltpu.make_async_copy(ybuf.at[from_slot, pl.ds(k, 1), :],
                                     y_hbm.at[pl.ds(dst_ref[tile * tm + k], 1), :], ssem.at[from_slot])

    def wait_in(slot_):
        pltpu.make_async_copy(xg_hbm.at[pl.ds(0, tm), :], xbuf.at[slot_], gsem.at[slot_]).wait()

    def wait_out(slot_):
        pltpu.make_async_copy(ybuf.at[slot_], y_hbm.at[pl.ds(0, tm), :], ssem.at[slot_]).wait()

    @pl.when((i == 0) & (e == 0))
    def _():
        def body(k, c):
            row_in(0, 0, k).start()
            return c
        lax.fori_loop(0, tm, body, 0)
        ybuf[1] = jnp.zeros((tm, D_MODEL), F32)

    @pl.when(e == 0)
    def _():
        wait_in(slot)
        xn_s[...] = xbuf[slot, :, :D_MODEL].astype(BF16)
        gate_s[...] = xbuf[slot, :, D_MODEL:]

    def start_neighbour_rows():
        for kk in range(quarter):
            k = e * quarter + kk
            row_in(i + 1, prev, k).start()
            row_out(i, prev, k).start()

    grp = tg_ref[i]

    @pl.when(grp >= 0)
    def _():
        start_neighbour_rows()
        contrib = _expert_ffn(xn_s[...], gate_s[...], grp * EXPERTS_PER_GROUP + e, wg_ref, wu_ref, wd_ref)

        @pl.when(e == 0)
        def _():
            acc_s[...] = contrib

        @pl.when(e > 0)
        def _():
            acc_s[...] += contrib

    @pl.when(grp < 0)
    def _():
        start_neighbour_rows()

    @pl.when(e == EXPERTS_PER_GROUP - 1)
    def _():
        @pl.when(i >= 1)
        def _():
            wait_out(slot)

        @pl.when(grp >= 0)
        def _():
            ybuf[slot] = acc_s[...]

        @pl.when(i == nt - 1)
        def _():
            wait_out(prev)
            wait_in(prev)


def moe_sparse(xg, grp, w_gate, w_up, w_down, layer, tm=512):
    d = D_MODEL
    src, dst_prev, tile_grp = _group_order(grp, tm)
    n_tiles = tile_grp.shape[0]
    expert_of = lambda i, e, s, t, tg: (layer, jnp.maximum(tg[i], 0) * EXPERTS_PER_GROUP + e, 0, 0)
    return pl.pallas_call(
        functools.partial(_moe_sparse_kernel, tm=tm),
        out_shape=jax.ShapeDtypeStruct((n_tiles * tm, d), F32),
        grid_spec=pltpu.PrefetchScalarGridSpec(
            num_scalar_prefetch=3,
            grid=(n_tiles, EXPERTS_PER_GROUP),
            in_specs=[pl.BlockSpec(memory_space=pl.ANY),
                      pl.BlockSpec((None, None, d, D_EXPERT), expert_of),
                      pl.BlockSpec((None, None, d, D_EXPERT), expert_of),
                      pl.BlockSpec((None, None, D_EXPERT, d), expert_of)],
            out_specs=pl.BlockSpec(memory_space=pl.ANY),
            scratch_shapes=[pltpu.VMEM((2, tm, PAYLOAD), F32), pltpu.VMEM((2, tm, d), F32),
                            pltpu.VMEM((tm, d), BF16), pltpu.VMEM((tm, LANES), F32),
                            pltpu.VMEM((tm, d), F32),
                            pltpu.SemaphoreType.DMA((2,)), pltpu.SemaphoreType.DMA((2,))]),
        compiler_params=_cparams("arbitrary", "arbitrary"),
        name="moe_sparse",
    )(src, dst_prev, tile_grp, xg, w_gate, w_up, w_down)


def _resid_add_kernel(x_ref, y_ref, gt_ref, o_ref):
    o_ref[...] = x_ref[...] + gt_ref[...] * y_ref[...]


def resid_add(rows, x, y, gate):
    gt_op, gt_spec = rows.mod_operand(gate)
    return pl.pallas_call(
        _resid_add_kernel,
        out_shape=jax.ShapeDtypeStruct(x.shape, F32),
        grid=(rows.n_tiles,),
        in_specs=[rows.row_spec(), rows.row_spec(), gt_spec],
        out_specs=rows.row_spec(),
        compiler_params=_cparams("parallel"),
        name="resid_add",
    )(x, y, gt_op)


def _kv_proj_kernel(x_ref, sh_ref, sc_ref, g_ref, wk_ref, wv_ref, wf_ref, bf_ref, kg_ref, gsum_ref,
                    gexp_ref, k_ref, v_ref, lf_ref, fc_ref, carry, *, rows):
    tm = rows.tm
    i = pl.program_id(0)
    xn = _modulate(x_ref[...], g_ref[...], sh_ref[...], sc_ref[...]).astype(BF16)
    k_ref[...] = _head_rms(jnp.dot(xn, wk_ref[...], preferred_element_type=F32), gsum_ref[...],
                           gexp_ref[...], kg_ref[...])
    v_ref[...] = jnp.dot(xn, wv_ref[...], preferred_element_type=F32)
    z = jnp.dot(xn, wf_ref[...], preferred_element_type=F32) + bf_ref[...]
    lf = -_softplus(-z)
    lf_ref[...] = lf[:, :N_HEADS]
    ri = lax.broadcasted_iota(jnp.int32, (tm, tm), 0)
    ci = lax.broadcasted_iota(jnp.int32, (tm, tm), 1)
    if rows.long:
        tri = (ci <= ri).astype(BF16)

        @pl.when(i % rows.tiles_per_seq == 0)
        def _():
            carry[...] = jnp.zeros_like(carry)
        cum = _split_dot_lhs_exact(tri, lf) + carry[...]
        carry[...] = cum[tm - 1:tm, :]
    else:
        tri = ((ci <= ri) & (ci // rows.t_len == ri // rows.t_len)).astype(BF16)
        cum = _split_dot_lhs_exact(tri, lf)
    fc_ref[...] = cum[:, :N_HEADS]


def _split_dot_lhs_exact(m, b):
    b1 = b.astype(BF16)
    r1 = b - b1.astype(F32)
    b2 = r1.astype(BF16)
    b3 = (r1 - b2.astype(F32)).astype(BF16)
    return (jnp.dot(m, b1, preferred_element_type=F32) + jnp.dot(m, b2, preferred_element_type=F32)
            + jnp.dot(m, b3, preferred_element_type=F32))


def kv_proj(rows, x, sh, sc, norm_g, kv_w, kv_b_f, k_norm_g):
    d = D_MODEL
    sh_op, sh_spec = rows.mod_operand(sh)
    sc_op, sc_spec = rows.mod_operand(sc)
    gsum, gexp = _group_mats()
    wk = kv_w[:, :d].astype(BF16)
    wv = kv_w[:, d:2 * d].astype(BF16)
    wf = jnp.pad(kv_w[:, 2 * d:], ((0, 0), (0, LANES - N_HEADS))).astype(BF16)
    bf = jnp.pad(kv_b_f, (0, LANES - N_HEADS)).reshape(1, LANES)
    kg = jnp.tile(k_norm_g, N_HEADS).reshape(1, d)
    consts = [norm_g.reshape(1, d), wk, wv, wf, bf, kg, gsum, gexp]
    act = jax.ShapeDtypeStruct((rows.rows, d), F32)
    small = jax.ShapeDtypeStruct((rows.rows, N_HEADS), F32)
    return pl.pallas_call(
        functools.partial(_kv_proj_kernel, rows=rows),
        out_shape=(act, act, small, small),
        grid=(rows.n_tiles,),
        in_specs=[rows.row_spec(), sh_spec, sc_spec] + [_full_spec(c) for c in consts],
        out_specs=(rows.row_spec(), rows.row_spec(), rows.row_spec(N_HEADS), rows.row_spec(N_HEADS)),
        scratch_shapes=[pltpu.VMEM((1, LANES), F32)],
        compiler_params=_cparams("arbitrary"),
        name="kv_proj",
    )(x, sh_op, sc_op, *consts)


def _q_proj_kernel(x_ref, sh_ref, sc_ref, g_ref, wq_ref, qg_ref, gsum_ref, gexp_ref, q_ref):
    xn = _modulate(x_ref[...], g_ref[...], sh_ref[...], sc_ref[...])
    q = _head_rms(_bdot(xn, wq_ref[...]), gsum_ref[...], gexp_ref[...], qg_ref[...])
    q_ref[...] = q * (HEAD_DIM ** -0.5 * LOG2E)


def q_proj(rows, x, sh, sc, norm_g, wq, q_norm_g):
    d = D_MODEL
    sh_op, sh_spec = rows.mod_operand(sh)
    sc_op, sc_spec = rows.mod_operand(sc)
    gsum, gexp = _group_mats()
    consts = [norm_g.reshape(1, d), wq.astype(BF16), jnp.tile(q_norm_g, N_HEADS).reshape(1, d), gsum, gexp]
    return pl.pallas_call(
        _q_proj_kernel,
        out_shape=jax.ShapeDtypeStruct((rows.rows, d), F32),
        grid=(rows.n_tiles,),
        in_specs=[rows.row_spec(), sh_spec, sc_spec] + [_full_spec(c) for c in consts],
        out_specs=rows.row_spec(),
        compiler_params=_cparams("parallel"),
        name="q_proj",
    )(x, sh_op, sc_op, *consts)


def _fox_prompt_kernel(q_ref, k_ref, v_ref, f_ref, o_ref, m_s, l_s, acc_s, *, tile):
    hp = pl.program_id(1)
    qi = pl.program_id(2)
    lane = lax.broadcasted_iota(jnp.int32, (1, LANES), 1)
    q = q_ref[...]
    qms = []
    for hh in range(2):
        qms.append(jnp.where((lane // HEAD_DIM) == hh, q, 0.0).astype(BF16))
        m_s[hh] = jnp.full((1, tile), NEG_INF, F32)
        l_s[hh] = jnp.zeros((1, tile), F32)
        acc_s[hh] = jnp.zeros((LANES, tile), F32)

    def step(ki, diagonal):
        start = pl.multiple_of(ki * tile, tile)
        kt = k_ref[pl.ds(start, tile), :].astype(BF16)
        vt = v_ref[pl.ds(start, tile), :].astype(BF16)
        fk = f_ref[pl.ds(start, tile), :]
        lane_h = lax.broadcasted_iota(jnp.int32, fk.shape, 1)
        for hh in range(2):
            f_key = jnp.sum(jnp.where(lane_h == 2 * hp + hh, fk, 0.0), axis=-1, keepdims=True) * LOG2E
            s = lax.dot_general(kt, qms[hh], (((1,), (1,)), ((), ())), preferred_element_type=F32) - f_key
            if diagonal:
                key = lax.broadcasted_iota(jnp.int32, (tile, tile), 0)
                qry = lax.broadcasted_iota(jnp.int32, (tile, tile), 1)
                s = jnp.where(key <= qry, s, NEG_INF)
            m_old = m_s[hh]
            m_new = jnp.maximum(m_old, jnp.max(s, axis=0, keepdims=True))
            p = jnp.exp2(s - m_new)
            alpha = jnp.exp2(m_old - m_new)
            l_s[hh] = alpha * l_s[hh] + jnp.sum(p, axis=0, keepdims=True)
            acc_s[hh] = alpha * acc_s[hh] + lax.dot_general(
                vt, p.astype(BF16), (((0,), (0,)), ((), ())), preferred_element_type=F32)
            m_s[hh] = m_new

    def body(ki, c):
        step(ki, False)
        return c

    lax.fori_loop(0, qi, body, 0)
    step(qi, True)
    chan = lax.broadcasted_iota(jnp.int32, (LANES, 1), 0)
    o_t = jnp.where((chan // HEAD_DIM) == 0, acc_s[0] / l_s[0], acc_s[1] / l_s[1])
    o_ref[...] = o_t.T


def fox_prompt_attention(q, k, v, fcum, bsz, t_len, tile=512):
    d = D_MODEL
    tile = min(tile, t_len)
    q3, k3, v3 = (z.reshape(bsz, t_len, d) for z in (q, k, v))
    f3 = fcum.reshape(bsz, t_len, N_HEADS)
    out = pl.pallas_call(
        functools.partial(_fox_prompt_kernel, tile=tile),
        out_shape=jax.ShapeDtypeStruct((bsz, t_len, d), F32),
        grid=(bsz, N_HEADS // 2, t_len // tile),
        in_specs=[pl.BlockSpec((None, tile, LANES), lambda b, h, i: (b, i, h)),
                  pl.BlockSpec((None, t_len, LANES), lambda b, h, i: (b, 0, h)),
                  pl.BlockSpec((None, t_len, LANES), lambda b, h, i: (b, 0, h)),
                  pl.BlockSpec((None, t_len, N_HEADS), lambda b, h, i: (b, 0, 0))],
        out_specs=pl.BlockSpec((None, tile, LANES), lambda b, h, i: (b, i, h)),
        scratch_shapes=[pltpu.VMEM((2, 1, tile), F32), pltpu.VMEM((2, 1, tile), F32),
                        pltpu.VMEM((2, LANES, tile), F32)],
        compiler_params=_cparams("parallel", "parallel", "arbitrary"),
        name="fox_prompt",
    )(q3, k3, v3, f3)
    return out.reshape(bsz * t_len, d)


def _fox_sample_kernel(pt_ref, q_ref, kn_ref, vn_ref, fnr_ref, *rest, t_len, page, pps):
    page_refs = rest[:3 * pps]
    o_ref, qbd_s, m_s, l_s, acc_s, rc_s = rest[3 * pps:]
    p = pl.program_id(1)
    nrow = t_len * N_HEADS
    head_of_row = lax.broadcasted_iota(jnp.int32, (nrow, D_MODEL), 0) % N_HEADS
    head_of_col = lax.broadcasted_iota(jnp.int32, (nrow, D_MODEL), 1) // HEAD_DIM
    diag = head_of_row == head_of_col

    @pl.when(p == 0)
    def _():
        q = q_ref[...]
        qrep = jnp.broadcast_to(q[:, None, :], (t_len, N_HEADS, D_MODEL)).reshape(nrow, D_MODEL)
        qbd = jnp.where(diag, qrep, 0.0)
        qbd_s[...] = qbd.astype(BF16)
        q_of_row = lax.broadcasted_iota(jnp.int32, (nrow, 1), 0) // N_HEADS
        fnr = fnr_ref[...] * LOG2E
        s_cols = []
        for t in range(t_len):
            s_t = jnp.sum(qbd * kn_ref[pl.ds(t, 1), :], axis=-1, keepdims=True)
            s_cols.append(jnp.where(q_of_row >= t, s_t - fnr[:, t:t + 1], NEG_INF))
        m = functools.reduce(jnp.maximum, s_cols)
        l = jnp.zeros((nrow, 1), F32)
        acc = jnp.zeros((nrow, D_MODEL), F32)
        for t in range(t_len):
            p_t = jnp.exp2(s_cols[t] - m)
            l = l + p_t
            acc = acc + p_t * vn_ref[pl.ds(t, 1), :]
        m_s[...] = m
        l_s[...] = l
        acc_s[...] = acc
        rc_s[...] = jnp.zeros_like(rc_s)

    ji = lax.broadcasted_iota(jnp.int32, (page, page), 0)
    si = lax.broadcasted_iota(jnp.int32, (page, page), 1)
    later = (ji > si).astype(BF16)
    qbd = qbd_s[...]
    rc = rc_s[...]
    s_parts = []
    for i in range(pps):
        k_ref, _, lf_ref = page_refs[3 * i:3 * i + 3]
        lf = lf_ref[...]
        r16 = _split_dot(lf, later) + rc
        rc = rc + jnp.sum(lf, axis=-1, keepdims=True)
        kt = k_ref[...].reshape(D_MODEL, page).astype(BF16)
        s = jnp.dot(qbd, kt, preferred_element_type=F32)
        s_parts.append(s + jnp.concatenate([r16 * LOG2E] * t_len, axis=0))
    rc_s[...] = rc
    m_old = m_s[...]
    m_new = functools.reduce(jnp.maximum, [m_old] + [jnp.max(s, axis=-1, keepdims=True) for s in s_parts])
    alpha = jnp.exp2(m_old - m_new)
    l = alpha * l_s[...]
    acc = alpha * acc_s[...]
    for i, s in enumerate(s_parts):
        pr = jnp.exp2(s - m_new)
        l = l + jnp.sum(pr, axis=-1, keepdims=True)
        vt = page_refs[3 * i + 1][...].reshape(D_MODEL, page).astype(BF16)
        acc = acc + lax.dot_general(pr.astype(BF16), vt, (((1,), (1,)), ((), ())),
                                    preferred_element_type=F32)
    l_s[...] = l
    acc_s[...] = acc
    m_s[...] = m_new

    @pl.when(p == pl.num_programs(1) - 1)
    def _():
        o = jnp.where(diag, acc_s[...] / l_s[...], 0.0)
        o_ref[...] = jnp.sum(o.reshape(t_len, N_HEADS, D_MODEL), axis=1)


def fox_sample_attention(q, k_new, v_new, fcum, cache_k, cache_v, cache_logf, page_table, bsz, t_len, pps=8):
    d = D_MODEL
    n_pool, page = cache_k.shape[:2]
    n_pages = page_table.shape[1]
    pps = min(pps, n_pages)
    assert n_pages % pps == 0
    nrow = t_len * N_HEADS
    q3, kn3, vn3 = (z.reshape(bsz, t_len, d) for z in (q, k_new, v_new))
    fc3 = fcum.reshape(bsz, t_len, N_HEADS)
    fnr = jnp.broadcast_to(jnp.transpose(fc3, (0, 2, 1))[:, None], (bsz, t_len, N_HEADS, t_len)).reshape(
        bsz, nrow, t_len)
    ck = jnp.transpose(cache_k, (0, 2, 3, 1))
    cv = jnp.transpose(cache_v, (0, 2, 3, 1))
    clf = jnp.transpose(cache_logf, (0, 2, 1))
    per_b = lambda shape: pl.BlockSpec((None,) + shape, lambda b, p, pt: (b,) + (0,) * len(shape))

    def paged(shape, i):
        return pl.BlockSpec((None,) + shape,
                            lambda b, p, pt: (pt[b, n_pages - 1 - (p * pps + i)],) + (0,) * len(shape))

    page_specs, page_ops = [], []
    for i in range(pps):
        page_specs += [paged((N_HEADS, HEAD_DIM, page), i), paged((N_HEADS, HEAD_DIM, page), i),
                       paged((N_HEADS, page), i)]
        page_ops += [ck, cv, clf]
    out = pl.pallas_call(
        functools.partial(_fox_sample_kernel, t_len=t_len, page=page, pps=pps),
        out_shape=jax.ShapeDtypeStruct((bsz, t_len, d), F32),
        grid_spec=pltpu.PrefetchScalarGridSpec(
            num_scalar_prefetch=1,
            grid=(bsz, n_pages // pps),
            in_specs=[per_b((t_len, d)), per_b((t_len, d)), per_b((t_len, d)), per_b((nrow, t_len))]
            + page_specs,
            out_specs=per_b((t_len, d)),
            scratch_shapes=[pltpu.VMEM((nrow, d), BF16), pltpu.VMEM((nrow, 1), F32),
                            pltpu.VMEM((nrow, 1), F32), pltpu.VMEM((nrow, d), F32),
                            pltpu.VMEM((N_HEADS, 1), F32)]),
        compiler_params=_cparams("parallel", "arbitrary"),
        name="fox_sample",
    )(page_table, q3, kn3, vn3, fnr, *page_ops)
    return out.reshape(bsz * t_len, d)


def _trunk(x3, mods, kv_mods, shift0, wkv0, attend, p, moe_w, moe_w32, depth=DEPTH):
    bsz, t_len, d = x3.shape
    x = x3.reshape(bsz * t_len, d)
    shifts, states = [], []
    v_first = None
    k_sh = v_sh = lf_sh = fcum = None
    for i in range(depth):
        precise = i < N_A and not _Rows(bsz, t_len, 256).long
        rows = _Rows(bsz, t_len, 256, precise)
        rows_moe = _Rows(bsz, t_len, 512, precise)
        ew = moe_w32 if precise else moe_w
        sh_m, sc_m, gt_m, sh_f, sc_f, gt_f = jnp.split(mods[i], 6, axis=-1)
        if i < N_A:
            in_rows = rows.long and bsz % LANE_BATCH == 0
            order = jnp.asarray(CHANNEL_ORDER, jnp.int32)
            cols = (lambda z: jnp.take(z, order, axis=-1)) if in_rows else (lambda z: z)
            vmix = None if i == 0 else (v_first, cols(p['a_v0'][i - 1]), p['a_v1'][i - 1], cols(p['a_v2'][i - 1]))
            r, w, k, v, a, g, last = rwkv_proj(
                rows, x, sh_m, sc_m, shift0[i], p['norm_g'][i, 0], p['a_mu'][i], cols(p['a_w0'][i]),
                cols(p['a_a0'][i]), cols(p['a_wr'][i]), cols(p['a_wk'][i]), cols(p['a_wv'][i]), p['a_w1'][i],
                cols(p['a_w2'][i]), p['a_a1'][i], cols(p['a_a2'][i]), p['a_g1'][i], cols(p['a_g2'][i]), vmix)
            if i == 0:
                v_first = v
            shifts.append(last)
            pars = [_param_lanes(z, bsz) for z in (p['a_kk'][i], p['a_ka'][i], p['a_rk'][i].reshape(-1),
                                                   p['a_lnx_w'][i], p['a_lnx_b'][i])]
            s0 = jnp.transpose(wkv0[i], (3, 2, 0, 1)).reshape(HEAD_DIM, HEAD_DIM, bsz * N_HEADS)
            if in_rows:
                y, s_t = wkv_scan_rows(*[z.reshape(bsz, t_len, d) for z in (r, w, k, v, a)], *pars, s0)
                x = resid_mm(rows, y.reshape(bsz * t_len, d), g, jnp.take(p['a_wo'][i], order, axis=0), x, gt_m)
            else:
                y, s_t = wkv_scan(*[_to_lanes(z, bsz, t_len) for z in (r, w, k, v, a)], *pars, s0)
                x = resid_mm(rows, _from_lanes(y, bsz, t_len), g, p['a_wo'][i], x, gt_m)
            states.append(jnp.transpose(s_t.reshape(HEAD_DIM, HEAD_DIM, bsz, N_HEADS), (2, 3, 1, 0)))
        else:
            j = i - N_A
            if j == 0:
                k_sh, v_sh, lf_sh, fcum = kv_proj(rows, x, kv_mods[0], kv_mods[1], p['kv_norm_g'], p['kv_w'],
                                                  p['kv_b_f'], p['k_norm_g'])
            q = q_proj(rows, x, sh_m, sc_m, p['norm_g'][i, 0], p['b_wq'][j], p['b_q_norm_g'][j])
            o = attend(q, k_sh, v_sh, fcum)
            x = resid_mm(rows, o, None, p['b_wo'][j], x, gt_m)
        if rows_moe.rows >= SPARSE_MOE_MIN_ROWS:
            xg, grp = moe_route(rows_moe, x, sh_f, sc_f, p['norm_g'][i, 1], p['router_w'], p['router_bias'])
            y = moe_sparse(xg, grp.reshape(-1), ew[0], ew[1], ew[2], i)
            x = resid_add(rows_moe, x, y, gt_f)
        else:
            x = moe_layer(rows_moe, x, sh_f, sc_f, gt_f, p['norm_g'][i, 1], p['router_w'], p['router_bias'],
                          ew[0], ew[1], ew[2], i)
    if depth < DEPTH:
        return x.reshape(bsz, t_len, d), jnp.stack(shifts)
    hd = (bsz, t_len, N_HEADS, HEAD_DIM)
    return (x.reshape(bsz, t_len, d), jnp.stack(shifts), jnp.stack(states), k_sh.reshape(hd),
            v_sh.reshape(hd), lf_sh.reshape(bsz, t_len, N_HEADS))


def kernel(x_prompt, x_sample, c_prompt, c_sample, state_shift, state_wkv, cache_k, cache_v, cache_logf, page_table, mod_w, mod_b, norm_g, a_mu, a_w0, a_w1, a_w2, a_a0, a_a1, a_a2, a_v0, a_v1, a_v2, a_g1, a_g2, a_kk, a_ka, a_rk, a_wr, a_wk, a_wv, a_wo, a_lnx_w, a_lnx_b, kv_mod_w, kv_mod_b, kv_norm_g, kv_w, kv_b_f, k_norm_g, b_wq, b_q_norm_g, b_wo, router_w, router_bias, moe_w_gate, moe_w_up, moe_w_down):
    p = dict(norm_g=norm_g, a_mu=a_mu, a_w0=a_w0, a_w1=a_w1, a_w2=a_w2, a_a0=a_a0, a_a1=a_a1, a_a2=a_a2,
             a_v0=a_v0, a_v1=a_v1, a_v2=a_v2, a_g1=a_g1, a_g2=a_g2, a_kk=a_kk, a_ka=a_ka, a_rk=a_rk,
             a_wr=a_wr, a_wk=a_wk, a_wv=a_wv, a_wo=a_wo, a_lnx_w=a_lnx_w, a_lnx_b=a_lnx_b,
             kv_norm_g=kv_norm_g, kv_w=kv_w, kv_b_f=kv_b_f, k_norm_g=k_norm_g, b_wq=b_wq,
             b_q_norm_g=b_q_norm_g, b_wo=b_wo, router_w=router_w, router_bias=router_bias)
    d = D_MODEL
    bp, t_p = x_prompt.shape[:2]
    bs, t_s = x_sample.shape[:2]
    moe_w = (moe_w_gate.astype(BF16), moe_w_up.astype(BF16), moe_w_down.astype(BF16))

    w_all = jnp.concatenate([mod_w[i] for i in range(DEPTH)] + [kv_mod_w], axis=1)
    b_all = jnp.concatenate([mod_b[i] for i in range(DEPTH)] + [kv_mod_b], axis=0)
    m_all = adaln_all(jnp.concatenate([c_prompt, c_sample], axis=0), w_all, b_all)

    def split_mods(m):
        layer = [m[:, i * 6 * d:(i + 1) * 6 * d] for i in range(DEPTH)]
        kvm = m[:, DEPTH * 6 * d:]
        return layer, (kvm[:, :d], kvm[:, d:])

    mods_p, kvm_p = split_mods(m_all[:bp])
    mods_s, kvm_s = split_mods(m_all[bp:])

    def attend_prompt(q, k, v, fcum):
        return fox_prompt_attention(q, k, v, fcum, bp, t_p)

    def attend_sample(q, k, v, fcum):
        return fox_sample_attention(q, k, v, fcum, cache_k, cache_v, cache_logf, page_table, bs, t_s)

    shift0_p = jnp.zeros((N_A, bp, d), F32)
    wkv0_p = jnp.zeros((N_A, bp, N_HEADS, HEAD_DIM, HEAD_DIM), F32)
    moe_w32 = (moe_w_gate, moe_w_up, moe_w_down)
    y_p, shift_p, wkv_p, k_p, v_p, lf_p = _trunk(x_prompt, mods_p, kvm_p, shift0_p, wkv0_p, attend_prompt,
                                                 p, moe_w, moe_w32)
    y_s, shift_s, wkv_s, k_s, v_s, lf_s = _trunk(x_sample, mods_s, kvm_s, state_shift, state_wkv,
                                                 attend_sample, p, moe_w, moe_w32)
    return (y_p, y_s, shift_p, wkv_p, k_p, v_p, lf_p, shift_s, wkv_s, k_s, v_s, lf_s)
```

```python
import functools

import jax
import jax.numpy as jnp
from jax import lax
from jax.experimental import pallas as pl
from jax.experimental.pallas import tpu as pltpu

D_MODEL = 1024
HEAD_DIM = 64
N_HEADS = D_MODEL // HEAD_DIM
N_EXPERTS = 16
N_GROUPS = 4
EXPERTS_PER_GROUP = N_EXPERTS // N_GROUPS
D_EXPERT = 512
DEPTH = 4
N_A = 2
RMS_EPS = 1e-6
GN_EPS = 64e-5
LANES = 128
VMEM_LIMIT = 56 * 1024 * 1024
SPARSE_MOE_MIN_ROWS = 4096

F32 = jnp.float32
BF16 = jnp.bfloat16
NEG_INF = float("-inf")
LOG2E = 1.4426950408889634


def _cparams(*sem):
    return pltpu.CompilerParams(dimension_semantics=sem, vmem_limit_bytes=VMEM_LIMIT)


def _bdot(a, b):
    return jnp.dot(a.astype(BF16), b.astype(BF16), preferred_element_type=F32)


def _dot3(a, b):
    a = a.astype(F32)
    b = b.astype(F32)
    a1 = a.astype(BF16)
    a2 = (a - a1.astype(F32)).astype(BF16)
    b1 = b.astype(BF16)
    b2 = (b - b1.astype(F32)).astype(BF16)
    return (jnp.dot(a1, b1, preferred_element_type=F32)
            + (jnp.dot(a1, b2, preferred_element_type=F32) + jnp.dot(a2, b1, preferred_element_type=F32)))


def _mm(a, b, precise):
    return _dot3(a, b) if precise else _bdot(a, b)


def _split_dot(a, m):
    a1 = a.astype(BF16)
    r1 = a - a1.astype(F32)
    a2 = r1.astype(BF16)
    a3 = (r1 - a2.astype(F32)).astype(BF16)
    return (jnp.dot(a1, m, preferred_element_type=F32) + jnp.dot(a2, m, preferred_element_type=F32)
            + jnp.dot(a3, m, preferred_element_type=F32))


def _sigmoid(z):
    return 1.0 / (1.0 + jnp.exp(-z))


def _silu(z):
    return z * _sigmoid(z)


def _softplus(z):
    return jnp.maximum(z, 0.0) + jnp.log(1.0 + jnp.exp(-jnp.abs(z)))


def _modulate(x, g, sh, sc):
    y = x * lax.rsqrt(jnp.mean(x * x, axis=-1, keepdims=True) + RMS_EPS)
    return (y * g) * (1.0 + sc) + sh


def _head_rms(z, gsum, gexp, g_row):
    ss = _split_dot(z * z, gsum)
    inv = lax.rsqrt(ss * (1.0 / HEAD_DIM) + RMS_EPS)
    return z * _split_dot(inv, gexp) * g_row


def _group_mats():
    head_of_lane = jnp.arange(D_MODEL) // HEAD_DIM
    gsum = (head_of_lane[:, None] == jnp.arange(LANES)[None, :]).astype(BF16)
    return gsum, gsum.T


class _Rows:
    def __init__(self, bsz, t_len, tile, precise=False):
        self.bsz, self.t_len = bsz, t_len
        self.precise = precise
        self.rows = bsz * t_len
        self.tm = min(tile, self.rows)
        self.long = t_len >= self.tm
        if self.long:
            assert t_len % self.tm == 0
        else:
            assert self.tm % t_len == 0
        assert self.rows % self.tm == 0
        self.n_tiles = self.rows // self.tm
        self.tiles_per_seq = t_len // self.tm if self.long else 1

    def row_spec(self, width=D_MODEL):
        return pl.BlockSpec((self.tm, width), lambda i, *_: (i, 0))

    def mod_operand(self, m):
        w = m.shape[-1]
        if self.long:
            tps = self.tiles_per_seq
            return m.reshape(self.bsz, 1, w), pl.BlockSpec((None, 1, w), lambda i, *_: (i // tps, 0, 0))
        return jnp.repeat(m, self.t_len, axis=0), pl.BlockSpec((self.tm, w), lambda i, *_: (i, 0))


def _full_spec(a):
    nd = a.ndim
    return pl.BlockSpec(a.shape, lambda *_: (0,) * nd)


def _adaln_kernel(c_ref, w_ref, b_ref, o_ref):
    c = c_ref[...]
    o_ref[...] = jnp.dot(_silu(c), w_ref[...], precision=lax.Precision.HIGHEST,
                         preferred_element_type=F32) + b_ref[...]


def adaln_all(c, w, b, tn=1024):
    m, k = c.shape
    n = w.shape[1]
    return pl.pallas_call(
        _adaln_kernel,
        out_shape=jax.ShapeDtypeStruct((m, n), F32),
        grid=(n // tn,),
        in_specs=[pl.BlockSpec((m, k), lambda j: (0, 0)), pl.BlockSpec((k, tn), lambda j: (0, j)),
                  pl.BlockSpec((1, tn), lambda j: (0, j))],
        out_specs=pl.BlockSpec((m, tn), lambda j: (0, j)),
        compiler_params=_cparams("parallel"),
        name="adaln",
    )(c, w, b.reshape(1, n))


def _rwkv_proj_kernel(*refs, rows, has_vmix):
    (x_ref, sh_ref, sc_ref, first_ref, g_ref, mu_ref, w0_ref, a0_ref,
     wr_ref, wk_ref, wv_ref, w1_ref, w2_ref, a1_ref, a2_ref, g1_ref, g2_ref) = refs[:17]
    pos = 17
    if has_vmix:
        vf_ref, v0_ref, v1_ref, v2_ref = refs[pos:pos + 4]
        pos += 4
    r_ref, w_ref, k_ref, v_ref, a_ref, gg_ref, last_ref = refs[pos:pos + 7]
    carry = refs[pos + 7]
    tm = rows.tm
    i = pl.program_id(0)

    xn = _modulate(x_ref[...], g_ref[...], sh_ref[...], sc_ref[...])
    rolled = pltpu.roll(xn, 1, 0)
    row = lax.broadcasted_iota(jnp.int32, (tm, 1), 0)
    if rows.long:
        @pl.when(i % rows.tiles_per_seq == 0)
        def _():
            carry[...] = first_ref[...]
        x_prev = jnp.where(row == 0, carry[...], rolled)
        carry[...] = xn[tm - 1:tm, :]
        last_ref[...] = xn[tm - 1:tm, :]
    else:
        x_prev = jnp.where(row % rows.t_len == 0, first_ref[...], rolled)
        last_ref[...] = xn
    xx = x_prev - xn
    mu = mu_ref[...]

    mm = functools.partial(_mm, precise=rows.precise)

    def mix(j):
        z = xn + xx * mu[j:j + 1, :]
        return z if rows.precise else z.astype(BF16)

    xv = mix(3)
    r_ref[...] = mm(mix(0), wr_ref[...])
    w_ref[...] = -_softplus(-(w0_ref[...] + mm(jnp.tanh(mm(mix(1), w1_ref[...])), w2_ref[...]))) - 0.5
    k_ref[...] = mm(mix(2), wk_ref[...])
    v = mm(xv, wv_ref[...])
    if has_vmix:
        v = v + (vf_ref[...] - v) * _sigmoid(v0_ref[...] + mm(mm(xv, v1_ref[...]), v2_ref[...]))
    v_ref[...] = v
    a_ref[...] = _sigmoid(a0_ref[...] + mm(mm(mix(4), a1_ref[...]), a2_ref[...]))
    gg_ref[...] = mm(_sigmoid(mm(mix(5), g1_ref[...])), g2_ref[...])


def rwkv_proj(rows, x, sh, sc, shift0, norm_g, mu, w0, a0, wr, wk, wv, w1, w2, a1, a2, g1, g2, vmix):
    d = D_MODEL
    sh_op, sh_spec = rows.mod_operand(sh)
    sc_op, sc_spec = rows.mod_operand(sc)
    first_op, first_spec = rows.mod_operand(shift0)
    row2 = lambda z: z.reshape(1, -1)
    wcast = (lambda z: z) if rows.precise else (lambda z: z.astype(BF16))
    ops = [x, sh_op, sc_op, first_op, row2(norm_g), mu, row2(w0), row2(a0)] + [
        wcast(z) for z in (wr, wk, wv, w1, w2, a1, a2, g1, g2)]
    specs = [rows.row_spec(), sh_spec, sc_spec, first_spec] + [_full_spec(o) for o in ops[4:]]
    if vmix is not None:
        v_first, v0, v1, v2 = vmix
        extra = [v_first, row2(v0), wcast(v1), wcast(v2)]
        ops += extra
        specs += [rows.row_spec()] + [_full_spec(o) for o in extra[1:]]
    act = jax.ShapeDtypeStruct((rows.rows, d), F32)
    if rows.long:
        tps = rows.tiles_per_seq
        last_shape = jax.ShapeDtypeStruct((rows.bsz, 1, d), F32)
        last_spec = pl.BlockSpec((None, 1, d), lambda i: (i // tps, 0, 0))
    else:
        last_shape, last_spec = act, rows.row_spec()
    outs = pl.pallas_call(
        functools.partial(_rwkv_proj_kernel, rows=rows, has_vmix=vmix is not None),
        out_shape=(act,) * 6 + (last_shape,),
        grid=(rows.n_tiles,),
        in_specs=specs,
        out_specs=(rows.row_spec(),) * 6 + (last_spec,),
        scratch_shapes=[pltpu.VMEM((1, d), F32)],
        compiler_params=_cparams("arbitrary"),
        name="rwkv_proj",
    )(*ops)
    last = outs[6]
    last = last.reshape(rows.bsz, d) if rows.long else last.reshape(rows.bsz, rows.t_len, d)[:, -1]
    return outs[:6] + (last,)


def _wkv_kernel(r_ref, w_ref, k_ref, v_ref, a_ref, kk_ref, ka_ref, rk_ref, lw_ref, lb_ref, s0_ref,
                y_ref, st_ref, S, nk_s, bb_s, km_s, dc_s, y_s, *, tt):
    tb = pl.program_id(1)

    @pl.when(tb == 0)
    def _():
        S[...] = s0_ref[...]

    k = k_ref[...]
    a = a_ref[...]
    kkr = k * kk_ref[...][None]
    ss = jnp.sum(kkr * kkr, axis=1, keepdims=True)
    kk = kkr * lax.rsqrt(jnp.maximum(ss, 1e-24))
    km = k * (1.0 + (a - 1.0) * ka_ref[...][None])
    nk_s[...] = -kk
    bb_s[...] = kk * a
    km_s[...] = km
    dc_s[...] = jnp.exp(-jnp.exp(w_ref[...]))

    def step(t, carry):
        def row(ref, j):
            return jnp.broadcast_to(ref[t, pl.ds(j, 1), :], (HEAD_DIM, LANES))

        sa = S[0] * row(nk_s, 0)
        for j in range(1, HEAD_DIM):
            sa = sa + S[j] * row(nk_s, j)
        vt = v_ref[t]
        y = None
        for j in range(HEAD_DIM):
            s_new = S[j] * row(dc_s, j) + sa * row(bb_s, j) + vt * row(km_s, j)
            S[j] = s_new
            yj = s_new * row(r_ref, j)
            y = yj if y is None else y + yj
        y_s[t] = y
        return carry

    lax.fori_loop(0, tt, step, 0)

    y = y_s[...]
    mean = jnp.mean(y, axis=1, keepdims=True)
    yc = y - mean
    var = jnp.mean(yc * yc, axis=1, keepdims=True)
    bonus = jnp.sum(r_ref[...] * km * rk_ref[...][None], axis=1, keepdims=True) * v_ref[...]
    y_ref[...] = yc * lax.rsqrt(var + GN_EPS) * lw_ref[...][None] + lb_ref[...][None] + bonus

    @pl.when(tb == pl.num_programs(1) - 1)
    def _():
        st_ref[...] = S[...]


def wkv_scan(r, w, k, v, a, kk_p, ka_p, rk_p, lw_p, lb_p, s0, *, tt=16):
    t_len, n, l_tot = r.shape
    tt = min(tt, t_len)
    assert t_len % tt == 0 and l_tot % LANES == 0
    seq_spec = pl.BlockSpec((tt, n, LANES), lambda g, t: (t, 0, g))
    par_spec = pl.BlockSpec((n, LANES), lambda g, t: (0, g))
    st_spec = pl.BlockSpec((n, n, LANES), lambda g, t: (0, 0, g))
    return pl.pallas_call(
        functools.partial(_wkv_kernel, tt=tt),
        out_shape=(jax.ShapeDtypeStruct(r.shape, F32), jax.ShapeDtypeStruct(s0.shape, F32)),
        grid=(l_tot // LANES, t_len // tt),
        in_specs=[seq_spec] * 5 + [par_spec] * 5 + [st_spec],
        out_specs=(seq_spec, st_spec),
        scratch_shapes=[pltpu.VMEM((n, n, LANES), F32)] + [pltpu.VMEM((tt, n, LANES), F32)] * 5,
        compiler_params=_cparams("parallel", "arbitrary"),
        name="wkv_scan",
    )(r, w, k, v, a, kk_p, ka_p, rk_p, lw_p, lb_p, s0)


LANE_BATCH = LANES // N_HEADS
CHANNEL_ORDER = tuple(int(h * HEAD_DIM + n) for n in range(HEAD_DIM) for h in range(N_HEADS))


def _wkv_rows_kernel(r_ref, w_ref, k_ref, v_ref, a_ref, kk_ref, ka_ref, rk_ref, lw_ref, lb_ref, s0_ref,
                     y_ref, st_ref, S, r_s, dc_s, km_s, v_s, bb_s, nk_s, y_s, *, tt):
    tb = pl.program_id(1)
    n = HEAD_DIM
    n_chunks = D_MODEL // LANES
    per_chunk = LANES // N_HEADS

    @pl.when(tb == 0)
    def _():
        S[...] = s0_ref[...]

    def to_lanes(src, dst, time_major):
        for c in range(n_chunks):
            xt = [src[b, :, c * LANES:(c + 1) * LANES].T for b in range(LANE_BATCH)]
            for i in range(per_chunk):
                m = jnp.concatenate([x[i * N_HEADS:(i + 1) * N_HEADS, :] for x in xt], axis=0)
                ch = c * per_chunk + i
                if time_major:
                    dst[pl.ds(ch, tt, stride=n), :] = m.T
                else:
                    dst[pl.ds(ch * tt, tt), :] = m.T

    def from_lanes(src, dst):
        for c in range(n_chunks):
            parts = [src[pl.ds(c * per_chunk + i, tt, stride=n), :].T for i in range(per_chunk)]
            for b in range(LANE_BATCH):
                xb = jnp.concatenate([p[b * N_HEADS:(b + 1) * N_HEADS, :] for p in parts], axis=0)
                dst[b, :, c * LANES:(c + 1) * LANES] = xb.T

    to_lanes(r_ref, r_s, False)
    to_lanes(w_ref, dc_s, False)
    to_lanes(k_ref, km_s, False)
    to_lanes(v_ref, v_s, True)
    to_lanes(a_ref, bb_s, False)

    def by_channel(z):
        return z.reshape(n, tt, LANES)

    def per_channel(p_ref):
        return p_ref[...][:, None, :]

    k = by_channel(km_s[...])
    a = by_channel(bb_s[...])
    kkr = k * per_channel(kk_ref)
    ss = jnp.sum(kkr * kkr, axis=0, keepdims=True)
    kk = kkr * lax.rsqrt(jnp.maximum(ss, 1e-24))
    km = k * (1.0 + (a - 1.0) * per_channel(ka_ref))
    rkk = jnp.sum(by_channel(r_s[...]) * km * per_channel(rk_ref), axis=0)
    nk_s[pl.ds(0, tt * n), :] = (-kk).reshape(n * tt, LANES)
    bb_s[...] = (kk * a).reshape(n * tt, LANES)
    km_s[...] = km.reshape(n * tt, LANES)
    dc_s[...] = jnp.exp(-jnp.exp(dc_s[...]))

    def row(ref, t, j):
        return jnp.broadcast_to(ref[pl.ds(j * tt + t, 1), :], (n, LANES))

    nk_s[pl.ds(tt * n, 8), :] = jnp.zeros((8, LANES), F32)
    sa0 = S[0] * row(nk_s, 0, 0)
    for j in range(1, n):
        sa0 = sa0 + S[j] * row(nk_s, 0, j)

    def step(t, sa):
        base = pl.multiple_of(t * n, n)
        vt = v_s[pl.ds(base, n), :]
        y = None
        sa_next = None
        for j in range(n):
            s_new = S[j] * row(dc_s, t, j) + sa * row(bb_s, t, j) + vt * row(km_s, t, j)
            S[j] = s_new
            yj = s_new * row(r_s, t, j)
            y = yj if y is None else y + yj
            sj = s_new * row(nk_s, t + 1, j)
            sa_next = sj if sa_next is None else sa_next + sj
        y_s[pl.ds(base, n), :] = y
        return sa_next

    lax.fori_loop(0, tt, step, sa0)

    y = y_s[...].reshape(tt, n, LANES)
    mean = jnp.mean(y, axis=1, keepdims=True)
    yc = y - mean
    var = jnp.mean(yc * yc, axis=1, keepdims=True)
    bonus = rkk[:, None, :] * v_s[...].reshape(tt, n, LANES)
    y_s[...] = (yc * lax.rsqrt(var + GN_EPS) * lw_ref[...][None] + lb_ref[...][None] + bonus).reshape(
        tt * n, LANES)
    from_lanes(y_s, y_ref)

    @pl.when(tb == pl.num_programs(1) - 1)
    def _():
        st_ref[...] = S[...]


def wkv_scan_rows(r, w, k, v, a, kk_p, ka_p, rk_p, lw_p, lb_p, s0, *, tt=64):
    bsz, t_len, d = r.shape
    n = HEAD_DIM
    assert bsz % LANE_BATCH == 0 and t_len % tt == 0
    seq_spec = pl.BlockSpec((LANE_BATCH, tt, d), lambda g, t: (g, t, 0))
    par_spec = pl.BlockSpec((n, LANES), lambda g, t: (0, g))
    st_spec = pl.BlockSpec((n, n, LANES), lambda g, t: (0, 0, g))
    return pl.pallas_call(
        functools.partial(_wkv_rows_kernel, tt=tt),
        out_shape=(jax.ShapeDtypeStruct(r.shape, F32), jax.ShapeDtypeStruct(s0.shape, F32)),
        grid=(bsz // LANE_BATCH, t_len // tt),
        in_specs=[seq_spec] * 5 + [par_spec] * 5 + [st_spec],
        out_specs=(seq_spec, st_spec),
        scratch_shapes=[pltpu.VMEM((n, n, LANES), F32)] + [pltpu.VMEM((tt * n, LANES), F32)] * 5
        + [pltpu.VMEM(((tt + 1) * n, LANES), F32), pltpu.VMEM((tt * n, LANES), F32)],
        compiler_params=_cparams("parallel", "arbitrary"),
        name="wkv_scan_rows",
    )(r, w, k, v, a, kk_p, ka_p, rk_p, lw_p, lb_p, s0)


def _to_lanes(z, bsz, t_len):
    return jnp.transpose(z.reshape(bsz, t_len, N_HEADS, HEAD_DIM), (1, 3, 0, 2)).reshape(
        t_len, HEAD_DIM, bsz * N_HEADS)


def _from_lanes(z, bsz, t_len):
    return jnp.transpose(z.reshape(t_len, HEAD_DIM, bsz, N_HEADS), (2, 0, 3, 1)).reshape(
        bsz * t_len, D_MODEL)


def _param_lanes(p, bsz):
    return jnp.broadcast_to(p.reshape(1, N_HEADS, HEAD_DIM), (bsz, N_HEADS, HEAD_DIM)).transpose(
        2, 0, 1).reshape(HEAD_DIM, bsz * N_HEADS)


def _resid_mm_kernel(*refs, has_mul, precise):
    if has_mul:
        a_ref, b_ref, w_ref, x_ref, gt_ref, o_ref = refs
        lhs = a_ref[...] * b_ref[...]
    else:
        a_ref, w_ref, x_ref, gt_ref, o_ref = refs
        lhs = a_ref[...]
    o_ref[...] = x_ref[...] + gt_ref[...] * _mm(lhs, w_ref[...], precise)


def resid_mm(rows, a, b, w, x, gate):
    gt_op, gt_spec = rows.mod_operand(gate)
    w = w if rows.precise else w.astype(BF16)
    ops = [a] + ([b] if b is not None else []) + [w, x, gt_op]
    specs = [rows.row_spec()] * (2 if b is not None else 1) + [_full_spec(w), rows.row_spec(), gt_spec]
    return pl.pallas_call(
        functools.partial(_resid_mm_kernel, has_mul=b is not None, precise=rows.precise),
        out_shape=jax.ShapeDtypeStruct(x.shape, F32),
        grid=(rows.n_tiles,),
        in_specs=specs,
        out_specs=rows.row_spec(),
        compiler_params=_cparams("parallel"),
        name="resid_mm",
    )(*ops)


def _route(probs, bias):
    tm = probs.shape[0]
    sel = probs + bias
    lane = lax.broadcasted_iota(jnp.int32, (tm, N_EXPERTS), 1)
    grp = lane // EXPERTS_PER_GROUP

    def top2(mask):
        x = jnp.where(mask, sel, NEG_INF)
        v1 = jnp.max(x, axis=-1, keepdims=True)
        i1 = jnp.min(jnp.where(x == v1, lane, N_EXPERTS), axis=-1, keepdims=True)
        x2 = jnp.where(lane == i1, NEG_INF, x)
        v2 = jnp.max(x2, axis=-1, keepdims=True)
        i2 = jnp.min(jnp.where(x2 == v2, lane, N_EXPERTS), axis=-1, keepdims=True)
        return v1 + v2, i1, i2

    best_score, _, _ = top2(grp == 0)
    best = jnp.zeros((tm, 1), jnp.int32)
    for g in range(1, N_GROUPS):
        score, _, _ = top2(grp == g)
        better = score > best_score
        best = jnp.where(better, g, best)
        best_score = jnp.where(better, score, best_score)
    _, i1, i2 = top2(grp == best)
    chosen = (lane == i1) | (lane == i2)
    w = jnp.where(chosen, probs, 0.0)
    return w / jnp.sum(w, axis=-1, keepdims=True), best


def _router_gate(xn, rw, rb):
    logits = jnp.dot(xn, rw, precision=lax.Precision.HIGHEST, preferred_element_type=F32)
    z = jnp.exp(logits - jnp.max(logits, axis=-1, keepdims=True))
    return _route(z / jnp.sum(z, axis=-1, keepdims=True), rb)


def _moe_kernel(x_ref, sh_ref, sc_ref, gt_ref, g_ref, rw_ref, rb_ref, wg_ref, wu_ref, wd_ref,
                o_ref, xn_s, gate_s, acc_s, *, precise):
    e = pl.program_id(1)

    @pl.when(e == 0)
    def _():
        xn = _modulate(x_ref[...], g_ref[...], sh_ref[...], sc_ref[...])
        xn_s[...] = xn.astype(xn_s.dtype)
        gate_s[...], _ = _router_gate(xn, rw_ref[...], rb_ref[...])
        acc_s[...] = jnp.zeros_like(acc_s)

    acc_s[...] += _expert_ffn(xn_s[...], gate_s[...], e, wg_ref, wu_ref, wd_ref, precise)

    @pl.when(e == pl.num_programs(1) - 1)
    def _():
        o_ref[...] = x_ref[...] + gt_ref[...] * acc_s[...]


def _expert_ffn(xn, gate, expert, wg_ref, wu_ref, wd_ref, precise=False):
    lane = lax.broadcasted_iota(jnp.int32, gate.shape, 1)
    g_e = jnp.sum(jnp.where(lane == expert, gate, 0.0), axis=-1, keepdims=True)
    h = _silu(_mm(xn, wg_ref[...], precise)) * _mm(xn, wu_ref[...], precise)
    return _mm(h * g_e, wd_ref[...], precise)


def moe_layer(rows, x, sh, sc, gate, norm_g, router_w, router_bias, w_gate, w_up, w_down, layer):
    d = D_MODEL
    sh_op, sh_spec = rows.mod_operand(sh)
    sc_op, sc_spec = rows.mod_operand(sc)
    gt_op, gt_spec = rows.mod_operand(gate)
    g2 = norm_g.reshape(1, d)
    rb = router_bias.reshape(1, N_EXPERTS)
    tm = rows.tm
    return pl.pallas_call(
        functools.partial(_moe_kernel, precise=rows.precise),
        out_shape=jax.ShapeDtypeStruct(x.shape, F32),
        grid=(rows.n_tiles, N_EXPERTS),
        in_specs=[rows.row_spec(), sh_spec, sc_spec, gt_spec, _full_spec(g2), _full_spec(router_w),
                  _full_spec(rb),
                  pl.BlockSpec((None, None, d, D_EXPERT), lambda i, e: (layer, e, 0, 0)),
                  pl.BlockSpec((None, None, d, D_EXPERT), lambda i, e: (layer, e, 0, 0)),
                  pl.BlockSpec((None, None, D_EXPERT, d), lambda i, e: (layer, e, 0, 0))],
        out_specs=rows.row_spec(),
        scratch_shapes=[pltpu.VMEM((tm, d), F32 if rows.precise else BF16),
                        pltpu.VMEM((tm, N_EXPERTS), F32), pltpu.VMEM((tm, d), F32)],
        compiler_params=_cparams("parallel", "arbitrary"),
        name="moe",
    )(x, sh_op, sc_op, gt_op, g2, router_w, rb, w_gate, w_up, w_down)


PAYLOAD = D_MODEL + LANES


def _moe_route_kernel(x_ref, sh_ref, sc_ref, g_ref, rw_ref, rb_ref, xg_ref, grp_ref):
    xn = _modulate(x_ref[...], g_ref[...], sh_ref[...], sc_ref[...])
    gate, best = _router_gate(xn, rw_ref[...], rb_ref[...])
    xg_ref[:, :D_MODEL] = xn
    xg_ref[:, D_MODEL:] = jnp.zeros((xn.shape[0], LANES), F32)
    xg_ref[:, D_MODEL:D_MODEL + N_EXPERTS] = gate
    grp_ref[...] = best


def moe_route(rows, x, sh, sc, norm_g, router_w, router_bias):
    d = D_MODEL
    sh_op, sh_spec = rows.mod_operand(sh)
    sc_op, sc_spec = rows.mod_operand(sc)
    g2 = norm_g.reshape(1, d)
    rb = router_bias.reshape(1, N_EXPERTS)
    return pl.pallas_call(
        _moe_route_kernel,
        out_shape=(jax.ShapeDtypeStruct((rows.rows, PAYLOAD), F32),
                   jax.ShapeDtypeStruct((rows.rows, 1), jnp.int32)),
        grid=(rows.n_tiles,),
        in_specs=[rows.row_spec(), sh_spec, sc_spec, _full_spec(g2), _full_spec(router_w), _full_spec(rb)],
        out_specs=(rows.row_spec(PAYLOAD), rows.row_spec(1)),
        compiler_params=_cparams("parallel"),
        name="moe_route",
    )(x, sh_op, sc_op, g2, router_w, rb)


def _group_order(grp, tm):
    n = grp.shape[0]
    p_slots = n + N_GROUPS * tm
    onehot = (grp[:, None] == jnp.arange(N_GROUPS, dtype=jnp.int32)[None, :]).astype(jnp.int32)
    counts = jnp.sum(onehot, axis=0)
    rank = jnp.sum((jnp.cumsum(onehot, axis=0) - onehot) * onehot, axis=1)
    padded = ((counts + tm - 1) // tm) * tm
    ends = jnp.cumsum(padded)
    pos = jnp.sum(onehot * (ends - padded)[None, :], axis=1) + rank
    token = jnp.arange(n, dtype=jnp.int32)
    filled = jnp.zeros((p_slots + 2 * tm,), jnp.int32).at[pos].set(token + 1)
    src = jnp.maximum(filled - 1, 0)
    real = jnp.minimum(filled[:p_slots], 1)
    dst = jnp.where(real == 1, src[:p_slots], n + jnp.cumsum(1 - real) - 1).astype(jnp.int32)
    dst_prev = jnp.concatenate([p_slots + jnp.arange(tm, dtype=jnp.int32), dst])
    tile_start = jnp.arange(p_slots // tm + 1, dtype=jnp.int32) * tm
    tile_grp = jnp.sum((tile_start[:, None] >= ends[None, :]).astype(jnp.int32), axis=1)
    tile_grp = jnp.where(tile_grp < N_GROUPS, tile_grp, -1).astype(jnp.int32)
    return src, dst_prev, tile_grp


def _moe_sparse_kernel(src_ref, dst_ref, tg_ref, xg_hbm, wg_ref, wu_ref, wd_ref, y_hbm,
                       xbuf, ybuf, xn_s, gate_s, acc_s, gsem, ssem, *, tm):
    i = pl.program_id(0)
    e = pl.program_id(1)
    nt = pl.num_programs(0)
    slot = i % 2
    prev = 1 - slot
    quarter = tm // EXPERTS_PER_GROUP

    def row_in(tile, to_slot, k):
        return pltpu.make_async_copy(xg_hbm.at[pl.ds(src_ref[tile * tm + k], 1), :],
                                     xbuf.at[to_slot, pl.ds(k, 1), :], gsem.at[to_slot])

    def row_out(tile, from_slot, k):
        return pltpu.make_async_copy(ybuf.at[from_slot, pl.ds(k, 1), :],
                                     y_hbm.at[pl.ds(dst_ref[tile * tm + k], 1), :], ssem.at[from_slot])

    def wait_in(slot_):
        pltpu.make_async_copy(xg_hbm.at[pl.ds(0, tm), :], xbuf.at[slot_], gsem.at[slot_]).wait()

    def wait_out(slot_):
        pltpu.make_async_copy(ybuf.at[slot_], y_hbm.at[pl.ds(0, tm), :], ssem.at[slot_]).wait()

    @pl.when((i == 0) & (e == 0))
    def _():
        def body(k, c):
            row_in(0, 0, k).start()
            return c
        lax.fori_loop(0, tm, body, 0)
        ybuf[1] = jnp.zeros((tm, D_MODEL), F32)

    @pl.when(e == 0)
    def _():
        wait_in(slot)
        xn_s[...] = xbuf[slot, :, :D_MODEL].astype(BF16)
        gate_s[...] = xbuf[slot, :, D_MODEL:]

    def start_neighbour_rows():
        for kk in range(quarter):
            k = e * quarter + kk
            row_in(i + 1, prev, k).start()
            row_out(i, prev, k).start()

    grp = tg_ref[i]

    @pl.when(grp >= 0)
    def _():
        start_neighbour_rows()
        contrib = _expert_ffn(xn_s[...], gate_s[...], grp * EXPERTS_PER_GROUP + e, wg_ref, wu_ref, wd_ref)

        @pl.when(e == 0)
        def _():
            acc_s[...] = contrib

        @pl.when(e > 0)
        def _():
            acc_s[...] += contrib

    @pl.when(grp < 0)
    def _():
        start_neighbour_rows()

    @pl.when(e == EXPERTS_PER_GROUP - 1)
    def _():
        @pl.when(i >= 1)
        def _():
            wait_out(slot)

        @pl.when(grp >= 0)
        def _():
            ybuf[slot] = acc_s[...]

        @pl.when(i == nt - 1)
        def _():
            wait_out(prev)
            wait_in(prev)


def moe_sparse(xg, grp, w_gate, w_up, w_down, layer, tm=512):
    d = D_MODEL
    src, dst_prev, tile_grp = _group_order(grp, tm)
    n_tiles = tile_grp.shape[0]
    expert_of = lambda i, e, s, t, tg: (layer, jnp.maximum(tg[i], 0) * EXPERTS_PER_GROUP + e, 0, 0)
    return pl.pallas_call(
        functools.partial(_moe_sparse_kernel, tm=tm),
        out_shape=jax.ShapeDtypeStruct((n_tiles * tm, d), F32),
        grid_spec=pltpu.PrefetchScalarGridSpec(
            num_scalar_prefetch=3,
            grid=(n_tiles, EXPERTS_PER_GROUP),
            in_specs=[pl.BlockSpec(memory_space=pl.ANY),
                      pl.BlockSpec((None, None, d, D_EXPERT), expert_of),
                      pl.BlockSpec((None, None, d, D_EXPERT), expert_of),
                      pl.BlockSpec((None, None, D_EXPERT, d), expert_of)],
            out_specs=pl.BlockSpec(memory_space=pl.ANY),
            scratch_shapes=[pltpu.VMEM((2, tm, PAYLOAD), F32), pltpu.VMEM((2, tm, d), F32),
                            pltpu.VMEM((tm, d), BF16), pltpu.VMEM((tm, LANES), F32),
                            pltpu.VMEM((tm, d), F32),
                            pltpu.SemaphoreType.DMA((2,)), pltpu.SemaphoreType.DMA((2,))]),
        compiler_params=_cparams("arbitrary", "arbitrary"),
        name="moe_sparse",
    )(src, dst_prev, tile_grp, xg, w_gate, w_up, w_down)


def _resid_add_kernel(x_ref, y_ref, gt_ref, o_ref):
    o_ref[...] = x_ref[...] + gt_ref[...] * y_ref[...]


def resid_add(rows, x, y, gate):
    gt_op, gt_spec = rows.mod_operand(gate)
    return pl.pallas_call(
        _resid_add_kernel,
        out_shape=jax.ShapeDtypeStruct(x.shape, F32),
        grid=(rows.n_tiles,),
        in_specs=[rows.row_spec(), rows.row_spec(), gt_spec],
        out_specs=rows.row_spec(),
        compiler_params=_cparams("parallel"),
        name="resid_add",
    )(x, y, gt_op)


def _kv_proj_kernel(x_ref, sh_ref, sc_ref, g_ref, wk_ref, wv_ref, wf_ref, bf_ref, kg_ref, gsum_ref,
                    gexp_ref, k_ref, v_ref, lf_ref, fc_ref, carry, *, rows):
    tm = rows.tm
    i = pl.program_id(0)
    xn = _modulate(x_ref[...], g_ref[...], sh_ref[...], sc_ref[...]).astype(BF16)
    k_ref[...] = _head_rms(jnp.dot(xn, wk_ref[...], preferred_element_type=F32), gsum_ref[...],
                           gexp_ref[...], kg_ref[...])
    v_ref[...] = jnp.dot(xn, wv_ref[...], preferred_element_type=F32)
    z = jnp.dot(xn, wf_ref[...], preferred_element_type=F32) + bf_ref[...]
    lf = -_softplus(-z)
    lf_ref[...] = lf[:, :N_HEADS]
    ri = lax.broadcasted_iota(jnp.int32, (tm, tm), 0)
    ci = lax.broadcasted_iota(jnp.int32, (tm, tm), 1)
    if rows.long:
        tri = (ci <= ri).astype(BF16)

        @pl.when(i % rows.tiles_per_seq == 0)
        def _():
            carry[...] = jnp.zeros_like(carry)
        cum = _split_dot_lhs_exact(tri, lf) + carry[...]
        carry[...] = cum[tm - 1:tm, :]
    else:
        tri = ((ci <= ri) & (ci // rows.t_len == ri // rows.t_len)).astype(BF16)
        cum = _split_dot_lhs_exact(tri, lf)
    fc_ref[...] = cum[:, :N_HEADS]


def _split_dot_lhs_exact(m, b):
    b1 = b.astype(BF16)
    r1 = b - b1.astype(F32)
    b2 = r1.astype(BF16)
    b3 = (r1 - b2.astype(F32)).astype(BF16)
    return (jnp.dot(m, b1, preferred_element_type=F32) + jnp.dot(m, b2, preferred_element_type=F32)
            + jnp.dot(m, b3, preferred_element_type=F32))


def kv_proj(rows, x, sh, sc, norm_g, kv_w, kv_b_f, k_norm_g):
    d = D_MODEL
    sh_op, sh_spec = rows.mod_operand(sh)
    sc_op, sc_spec = rows.mod_operand(sc)
    gsum, gexp = _group_mats()
    wk = kv_w[:, :d].astype(BF16)
    wv = kv_w[:, d:2 * d].astype(BF16)
    wf = jnp.pad(kv_w[:, 2 * d:], ((0, 0), (0, LANES - N_HEADS))).astype(BF16)
    bf = jnp.pad(kv_b_f, (0, LANES - N_HEADS)).reshape(1, LANES)
    kg = jnp.tile(k_norm_g, N_HEADS).reshape(1, d)
    consts = [norm_g.reshape(1, d), wk, wv, wf, bf, kg, gsum, gexp]
    act = jax.ShapeDtypeStruct((rows.rows, d), F32)
    small = jax.ShapeDtypeStruct((rows.rows, N_HEADS), F32)
    return pl.pallas_call(
        functools.partial(_kv_proj_kernel, rows=rows),
        out_shape=(act, act, small, small),
        grid=(rows.n_tiles,),
        in_specs=[rows.row_spec(), sh_spec, sc_spec] + [_full_spec(c) for c in consts],
        out_specs=(rows.row_spec(), rows.row_spec(), rows.row_spec(N_HEADS), rows.row_spec(N_HEADS)),
        scratch_shapes=[pltpu.VMEM((1, LANES), F32)],
        compiler_params=_cparams("arbitrary"),
        name="kv_proj",
    )(x, sh_op, sc_op, *consts)


def _q_proj_kernel(x_ref, sh_ref, sc_ref, g_ref, wq_ref, qg_ref, gsum_ref, gexp_ref, q_ref):
    xn = _modulate(x_ref[...], g_ref[...], sh_ref[...], sc_ref[...])
    q = _head_rms(_bdot(xn, wq_ref[...]), gsum_ref[...], gexp_ref[...], qg_ref[...])
    q_ref[...] = q * (HEAD_DIM ** -0.5 * LOG2E)


def q_proj(rows, x, sh, sc, norm_g, wq, q_norm_g):
    d = D_MODEL
    sh_op, sh_spec = rows.mod_operand(sh)
    sc_op, sc_spec = rows.mod_operand(sc)
    gsum, gexp = _group_mats()
    consts = [norm_g.reshape(1, d), wq.astype(BF16), jnp.tile(q_norm_g, N_HEADS).reshape(1, d), gsum, gexp]
    return pl.pallas_call(
        _q_proj_kernel,
        out_shape=jax.ShapeDtypeStruct((rows.rows, d), F32),
        grid=(rows.n_tiles,),
        in_specs=[rows.row_spec(), sh_spec, sc_spec] + [_full_spec(c) for c in consts],
        out_specs=rows.row_spec(),
        compiler_params=_cparams("parallel"),
        name="q_proj",
    )(x, sh_op, sc_op, *consts)


def _fox_prompt_kernel(q_ref, k_ref, v_ref, f_ref, o_ref, m_s, l_s, acc_s, *, tile):
    hp = pl.program_id(1)
    qi = pl.program_id(2)
    lane = lax.broadcasted_iota(jnp.int32, (1, LANES), 1)
    q = q_ref[...]
    qms = []
    for hh in range(2):
        qms.append(jnp.where((lane // HEAD_DIM) == hh, q, 0.0).astype(BF16))
        m_s[hh] = jnp.full((1, tile), NEG_INF, F32)
        l_s[hh] = jnp.zeros((1, tile), F32)
        acc_s[hh] = jnp.zeros((LANES, tile), F32)

    def step(ki, diagonal):
        start = pl.multiple_of(ki * tile, tile)
        kt = k_ref[pl.ds(start, tile), :].astype(BF16)
        vt = v_ref[pl.ds(start, tile), :].astype(BF16)
        fk = f_ref[pl.ds(start, tile), :]
        lane_h = lax.broadcasted_iota(jnp.int32, fk.shape, 1)
        scores = []
        for hh in range(2):
            f_key = jnp.sum(jnp.where(lane_h == 2 * hp + hh, fk, 0.0), axis=-1, keepdims=True) * LOG2E
            s = lax.dot_general(kt, qms[hh], (((1,), (1,)), ((), ())), preferred_element_type=F32) - f_key
            if diagonal:
                key = lax.broadcasted_iota(jnp.int32, (tile, tile), 0)
                qry = lax.broadcasted_iota(jnp.int32, (tile, tile), 1)
                s = jnp.where(key <= qry, s, NEG_INF)
            scores.append(s)
        probs = []
        for hh in range(2):
            m_old = m_s[hh]
            m_new = jnp.maximum(m_old, jnp.max(scores[hh], axis=0, keepdims=True))
            p = jnp.exp2(scores[hh] - m_new)
            alpha = jnp.exp2(m_old - m_new)
            l_s[hh] = alpha * l_s[hh] + jnp.sum(p, axis=0, keepdims=True)
            m_s[hh] = m_new
            probs.append((alpha, p.astype(BF16)))
        for hh in range(2):
            alpha, p = probs[hh]
            acc_s[hh] = alpha * acc_s[hh] + lax.dot_general(
                vt, p, (((0,), (0,)), ((), ())), preferred_element_type=F32)

    def body(ki, c):
        step(ki, False)
        return c

    lax.fori_loop(0, qi, body, 0)
    step(qi, True)
    chan = lax.broadcasted_iota(jnp.int32, (LANES, 1), 0)
    o_t = jnp.where((chan // HEAD_DIM) == 0, acc_s[0] / l_s[0], acc_s[1] / l_s[1])
    o_ref[...] = o_t.T


def fox_prompt_attention(q, k, v, fcum, bsz, t_len, tile=512):
    d = D_MODEL
    tile = min(tile, t_len)
    q3, k3, v3 = (z.reshape(bsz, t_len, d) for z in (q, k, v))
    f3 = fcum.reshape(bsz, t_len, N_HEADS)
    out = pl.pallas_call(
        functools.partial(_fox_prompt_kernel, tile=tile),
        out_shape=jax.ShapeDtypeStruct((bsz, t_len, d), F32),
        grid=(bsz, N_HEADS // 2, t_len // tile),
        in_specs=[pl.BlockSpec((None, tile, LANES), lambda b, h, i: (b, i, h)),
                  pl.BlockSpec((None, t_len, LANES), lambda b, h, i: (b, 0, h)),
                  pl.BlockSpec((None, t_len, LANES), lambda b, h, i: (b, 0, h)),
                  pl.BlockSpec((None, t_len, N_HEADS), lambda b, h, i: (b, 0, 0))],
        out_specs=pl.BlockSpec((None, tile, LANES), lambda b, h, i: (b, i, h)),
        scratch_shapes=[pltpu.VMEM((2, 1, tile), F32), pltpu.VMEM((2, 1, tile), F32),
                        pltpu.VMEM((2, LANES, tile), F32)],
        compiler_params=_cparams("parallel", "parallel", "arbitrary"),
        name="fox_prompt",
    )(q3, k3, v3, f3)
    return out.reshape(bsz * t_len, d)


def _fox_sample_kernel(pt_ref, q_ref, kn_ref, vn_ref, fnr_ref, *rest, t_len, page, pps):
    page_refs = rest[:3 * pps]
    o_ref, qbd_s, m_s, l_s, acc_s, rc_s = rest[3 * pps:]
    p = pl.program_id(1)
    nrow = t_len * N_HEADS
    head_of_row = lax.broadcasted_iota(jnp.int32, (nrow, D_MODEL), 0) % N_HEADS
    head_of_col = lax.broadcasted_iota(jnp.int32, (nrow, D_MODEL), 1) // HEAD_DIM
    diag = head_of_row == head_of_col

    @pl.when(p == 0)
    def _():
        q = q_ref[...]
        qrep = jnp.broadcast_to(q[:, None, :], (t_len, N_HEADS, D_MODEL)).reshape(nrow, D_MODEL)
        qbd = jnp.where(diag, qrep, 0.0)
        qbd_s[...] = qbd.astype(BF16)
        q_of_row = lax.broadcasted_iota(jnp.int32, (nrow, 1), 0) // N_HEADS
        fnr = fnr_ref[...] * LOG2E
        s_cols = []
        for t in range(t_len):
            s_t = jnp.sum(qbd * kn_ref[pl.ds(t, 1), :], axis=-1, keepdims=True)
            s_cols.append(jnp.where(q_of_row >= t, s_t - fnr[:, t:t + 1], NEG_INF))
        m = functools.reduce(jnp.maximum, s_cols)
        l = jnp.zeros((nrow, 1), F32)
        acc = jnp.zeros((nrow, D_MODEL), F32)
        for t in range(t_len):
            p_t = jnp.exp2(s_cols[t] - m)
            l = l + p_t
            acc = acc + p_t * vn_ref[pl.ds(t, 1), :]
        m_s[...] = m
        l_s[...] = l
        acc_s[...] = acc
        rc_s[...] = jnp.zeros_like(rc_s)

    ji = lax.broadcasted_iota(jnp.int32, (page, page), 0)
    si = lax.broadcasted_iota(jnp.int32, (page, page), 1)
    later = (ji > si).astype(BF16)
    qbd = qbd_s[...]
    rc = rc_s[...]
    s_parts = []
    for i in range(pps):
        k_ref, _, lf_ref = page_refs[3 * i:3 * i + 3]
        lf = lf_ref[...]
        r16 = _split_dot(lf, later) + rc
        rc = rc + jnp.sum(lf, axis=-1, keepdims=True)
        kt = k_ref[...].reshape(D_MODEL, page).astype(BF16)
        s = jnp.dot(qbd, kt, preferred_element_type=F32)
        s_parts.append(s + jnp.concatenate([r16 * LOG2E] * t_len, axis=0))
    rc_s[...] = rc
    m_old = m_s[...]
    m_new = functools.reduce(jnp.maximum, [m_old] + [jnp.max(s, axis=-1, keepdims=True) for s in s_parts])
    alpha = jnp.exp2(m_old - m_new)
    l = alpha * l_s[...]
    acc = alpha * acc_s[...]
    for i, s in enumerate(s_parts):
        pr = jnp.exp2(s - m_new)
        l = l + jnp.sum(pr, axis=-1, keepdims=True)
        vt = page_refs[3 * i + 1][...].reshape(D_MODEL, page).astype(BF16)
        acc = acc + lax.dot_general(pr.astype(BF16), vt, (((1,), (1,)), ((), ())),
                                    preferred_element_type=F32)
    l_s[...] = l
    acc_s[...] = acc
    m_s[...] = m_new

    @pl.when(p == pl.num_programs(1) - 1)
    def _():
        o = jnp.where(diag, acc_s[...] / l_s[...], 0.0)
        o_ref[...] = jnp.sum(o.reshape(t_len, N_HEADS, D_MODEL), axis=1)


def fox_sample_attention(q, k_new, v_new, fcum, cache_k, cache_v, cache_logf, page_table, bsz, t_len, pps=8):
    d = D_MODEL
    n_pool, page = cache_k.shape[:2]
    n_pages = page_table.shape[1]
    pps = min(pps, n_pages)
    assert n_pages % pps == 0
    nrow = t_len * N_HEADS
    q3, kn3, vn3 = (z.reshape(bsz, t_len, d) for z in (q, k_new, v_new))
    fc3 = fcum.reshape(bsz, t_len, N_HEADS)
    fnr = jnp.broadcast_to(jnp.transpose(fc3, (0, 2, 1))[:, None], (bsz, t_len, N_HEADS, t_len)).reshape(
        bsz, nrow, t_len)
    ck = jnp.transpose(cache_k, (0, 2, 3, 1))
    cv = jnp.transpose(cache_v, (0, 2, 3, 1))
    clf = jnp.transpose(cache_logf, (0, 2, 1))
    per_b = lambda shape: pl.BlockSpec((None,) + shape, lambda b, p, pt: (b,) + (0,) * len(shape))

    def paged(shape, i):
        return pl.BlockSpec((None,) + shape,
                            lambda b, p, pt: (pt[b, n_pages - 1 - (p * pps + i)],) + (0,) * len(shape))

    page_specs, page_ops = [], []
    for i in range(pps):
        page_specs += [paged((N_HEADS, HEAD_DIM, page), i), paged((N_HEADS, HEAD_DIM, page), i),
                       paged((N_HEADS, page), i)]
        page_ops += [ck, cv, clf]
    out = pl.pallas_call(
        functools.partial(_fox_sample_kernel, t_len=t_len, page=page, pps=pps),
        out_shape=jax.ShapeDtypeStruct((bsz, t_len, d), F32),
        grid_spec=pltpu.PrefetchScalarGridSpec(
            num_scalar_prefetch=1,
            grid=(bsz, n_pages // pps),
            in_specs=[per_b((t_len, d)), per_b((t_len, d)), per_b((t_len, d)), per_b((nrow, t_len))]
            + page_specs,
            out_specs=per_b((t_len, d)),
            scratch_shapes=[pltpu.VMEM((nrow, d), BF16), pltpu.VMEM((nrow, 1), F32),
                            pltpu.VMEM((nrow, 1), F32), pltpu.VMEM((nrow, d), F32),
                            pltpu.VMEM((N_HEADS, 1), F32)]),
        compiler_params=_cparams("parallel", "arbitrary"),
        name="fox_sample",
    )(page_table, q3, kn3, vn3, fnr, *page_ops)
    return out.reshape(bsz * t_len, d)


def _trunk(x3, mods, kv_mods, shift0, wkv0, attend, p, moe_w, moe_w32, depth=DEPTH):
    bsz, t_len, d = x3.shape
    x = x3.reshape(bsz * t_len, d)
    shifts, states = [], []
    v_first = None
    k_sh = v_sh = lf_sh = fcum = None
    for i in range(depth):
        precise = i < N_A and not _Rows(bsz, t_len, 256).long
        rows = _Rows(bsz, t_len, 256, precise)
        rows_moe = _Rows(bsz, t_len, 512, precise)
        ew = moe_w32 if precise else moe_w
        sh_m, sc_m, gt_m, sh_f, sc_f, gt_f = jnp.split(mods[i], 6, axis=-1)
        if i < N_A:
            in_rows = rows.long and bsz % LANE_BATCH == 0
            order = jnp.asarray(CHANNEL_ORDER, jnp.int32)
            cols = (lambda z: jnp.take(z, order, axis=-1)) if in_rows else (lambda z: z)
            vmix = None if i == 0 else (v_first, cols(p['a_v0'][i - 1]), p['a_v1'][i - 1], cols(p['a_v2'][i - 1]))
            r, w, k, v, a, g, last = rwkv_proj(
                rows, x, sh_m, sc_m, shift0[i], p['norm_g'][i, 0], p['a_mu'][i], cols(p['a_w0'][i]),
                cols(p['a_a0'][i]), cols(p['a_wr'][i]), cols(p['a_wk'][i]), cols(p['a_wv'][i]), p['a_w1'][i],
                cols(p['a_w2'][i]), p['a_a1'][i], cols(p['a_a2'][i]), p['a_g1'][i], cols(p['a_g2'][i]), vmix)
            if i == 0:
                v_first = v
            shifts.append(last)
            pars = [_param_lanes(z, bsz) for z in (p['a_kk'][i], p['a_ka'][i], p['a_rk'][i].reshape(-1),
                                                   p['a_lnx_w'][i], p['a_lnx_b'][i])]
            s0 = jnp.transpose(wkv0[i], (3, 2, 0, 1)).reshape(HEAD_DIM, HEAD_DIM, bsz * N_HEADS)
            if in_rows:
                y, s_t = wkv_scan_rows(*[z.reshape(bsz, t_len, d) for z in (r, w, k, v, a)], *pars, s0)
                x = resid_mm(rows, y.reshape(bsz * t_len, d), g, jnp.take(p['a_wo'][i], order, axis=0), x, gt_m)
            else:
                y, s_t = wkv_scan(*[_to_lanes(z, bsz, t_len) for z in (r, w, k, v, a)], *pars, s0)
                x = resid_mm(rows, _from_lanes(y, bsz, t_len), g, p['a_wo'][i], x, gt_m)
            states.append(jnp.transpose(s_t.reshape(HEAD_DIM, HEAD_DIM, bsz, N_HEADS), (2, 3, 1, 0)))
        else:
            j = i - N_A
            if j == 0:
                k_sh, v_sh, lf_sh, fcum = kv_proj(rows, x, kv_mods[0], kv_mods[1], p['kv_norm_g'], p['kv_w'],
                                                  p['kv_b_f'], p['k_norm_g'])
            q = q_proj(rows, x, sh_m, sc_m, p['norm_g'][i, 0], p['b_wq'][j], p['b_q_norm_g'][j])
            o = attend(q, k_sh, v_sh, fcum)
            x = resid_mm(rows, o, None, p['b_wo'][j], x, gt_m)
        if rows_moe.rows >= SPARSE_MOE_MIN_ROWS:
            xg, grp = moe_route(rows_moe, x, sh_f, sc_f, p['norm_g'][i, 1], p['router_w'], p['router_bias'])
            y = moe_sparse(xg, grp.reshape(-1), ew[0], ew[1], ew[2], i)
            x = resid_add(rows_moe, x, y, gt_f)
        else:
            x = moe_layer(rows_moe, x, sh_f, sc_f, gt_f, p['norm_g'][i, 1], p['router_w'], p['router_bias'],
                          ew[0], ew[1], ew[2], i)
    if depth < DEPTH:
        return x.reshape(bsz, t_len, d), jnp.stack(shifts)
    hd = (bsz, t_len, N_HEADS, HEAD_DIM)
    return (x.reshape(bsz, t_len, d), jnp.stack(shifts), jnp.stack(states), k_sh.reshape(hd),
            v_sh.reshape(hd), lf_sh.reshape(bsz, t_len, N_HEADS))


def kernel(x_prompt, x_sample, c_prompt, c_sample, state_shift, state_wkv, cache_k, cache_v, cache_logf, page_table, mod_w, mod_b, norm_g, a_mu, a_w0, a_w1, a_w2, a_a0, a_a1, a_a2, a_v0, a_v1, a_v2, a_g1, a_g2, a_kk, a_ka, a_rk, a_wr, a_wk, a_wv, a_wo, a_lnx_w, a_lnx_b, kv_mod_w, kv_mod_b, kv_norm_g, kv_w, kv_b_f, k_norm_g, b_wq, b_q_norm_g, b_wo, router_w, router_bias, moe_w_gate, moe_w_up, moe_w_down):
    p = dict(norm_g=norm_g, a_mu=a_mu, a_w0=a_w0, a_w1=a_w1, a_w2=a_w2, a_a0=a_a0, a_a1=a_a1, a_a2=a_a2,
             a_v0=a_v0, a_v1=a_v1, a_v2=a_v2, a_g1=a_g1, a_g2=a_g2, a_kk=a_kk, a_ka=a_ka, a_rk=a_rk,
             a_wr=a_wr, a_wk=a_wk, a_wv=a_wv, a_wo=a_wo, a_lnx_w=a_lnx_w, a_lnx_b=a_lnx_b,
             kv_norm_g=kv_norm_g, kv_w=kv_w, kv_b_f=kv_b_f, k_norm_g=k_norm_g, b_wq=b_wq,
             b_q_norm_g=b_q_norm_g, b_wo=b_wo, router_w=router_w, router_bias=router_bias)
    d = D_MODEL
    bp, t_p = x_prompt.shape[:2]
    bs, t_s = x_sample.shape[:2]
    moe_w = (moe_w_gate.astype(BF16), moe_w_up.astype(BF16), moe_w_down.astype(BF16))

    w_all = jnp.concatenate([mod_w[i] for i in range(DEPTH)] + [kv_mod_w], axis=1)
    b_all = jnp.concatenate([mod_b[i] for i in range(DEPTH)] + [kv_mod_b], axis=0)
    m_all = adaln_all(jnp.concatenate([c_prompt, c_sample], axis=0), w_all, b_all)

    def split_mods(m):
        layer = [m[:, i * 6 * d:(i + 1) * 6 * d] for i in range(DEPTH)]
        kvm = m[:, DEPTH * 6 * d:]
        return layer, (kvm[:, :d], kvm[:, d:])

    mods_p, kvm_p = split_mods(m_all[:bp])
    mods_s, kvm_s = split_mods(m_all[bp:])

    def attend_prompt(q, k, v, fcum):
        return fox_prompt_attention(q, k, v, fcum, bp, t_p)

    def attend_sample(q, k, v, fcum):
        return fox_sample_attention(q, k, v, fcum, cache_k, cache_v, cache_logf, page_table, bs, t_s)

    shift0_p = jnp.zeros((N_A, bp, d), F32)
    wkv0_p = jnp.zeros((N_A, bp, N_HEADS, HEAD_DIM, HEAD_DIM), F32)
    moe_w32 = (moe_w_gate, moe_w_up, moe_w_down)
    y_p, shift_p, wkv_p, k_p, v_p, lf_p = _trunk(x_prompt, mods_p, kvm_p, shift0_p, wkv0_p, attend_prompt,
                                                 p, moe_w, moe_w32)
    y_s, shift_s, wkv_s, k_s, v_s, lf_s = _trunk(x_sample, mods_s, kvm_s, state_shift, state_wkv,
                                                 attend_sample, p, moe_w, moe_w32)
    return (y_p, y_s, shift_p, wkv_p, k_p, v_p, lf_p, shift_s, wkv_s, k_s, v_s, lf_s)
```

```python
import functools

import jax
import jax.numpy as jnp
from jax import lax
from jax.experimental import pallas as pl
from jax.experimental.pallas import tpu as pltpu

D_MODEL = 1024
HEAD_DIM = 64
N_HEADS = D_MODEL // HEAD_DIM
N_EXPERTS = 16
N_GROUPS = 4
EXPERTS_PER_GROUP = N_EXPERTS // N_GROUPS
D_EXPERT = 512
DEPTH = 4
N_A = 2
RMS_EPS = 1e-6
GN_EPS = 64e-5
LANES = 128
VMEM_LIMIT = 56 * 1024 * 1024
SPARSE_MOE_MIN_ROWS = 4096

F32 = jnp.float32
BF16 = jnp.bfloat16
NEG_INF = float("-inf")
LOG2E = 1.4426950408889634


def _cparams(*sem):
    return pltpu.CompilerParams(dimension_semantics=sem, vmem_limit_bytes=VMEM_LIMIT)


def _bdot(a, b):
    return jnp.dot(a.astype(BF16), b.astype(BF16), preferred_element_type=F32)


def _dot3(a, b):
    a = a.astype(F32)
    b = b.astype(F32)
    a1 = a.astype(BF16)
    a2 = (a - a1.astype(F32)).astype(BF16)
    b1 = b.astype(BF16)
    b2 = (b - b1.astype(F32)).astype(BF16)
    return (jnp.dot(a1, b1, preferred_element_type=F32)
            + (jnp.dot(a1, b2, preferred_element_type=F32) + jnp.dot(a2, b1, preferred_element_type=F32)))


def _mm(a, b, precise):
    return _dot3(a, b) if precise else _bdot(a, b)


def _split_dot(a, m):
    a1 = a.astype(BF16)
    r1 = a - a1.astype(F32)
    a2 = r1.astype(BF16)
    a3 = (r1 - a2.astype(F32)).astype(BF16)
    return (jnp.dot(a1, m, preferred_element_type=F32) + jnp.dot(a2, m, preferred_element_type=F32)
            + jnp.dot(a3, m, preferred_element_type=F32))


def _sigmoid(z):
    return 1.0 / (1.0 + jnp.exp(-z))


def _silu(z):
    return z * _sigmoid(z)


def _softplus(z):
    return jnp.maximum(z, 0.0) + jnp.log(1.0 + jnp.exp(-jnp.abs(z)))


def _modulate(x, g, sh, sc):
    y = x * lax.rsqrt(jnp.mean(x * x, axis=-1, keepdims=True) + RMS_EPS)
    return (y * g) * (1.0 + sc) + sh


def _head_rms(z, gsum, gexp, g_row):
    ss = _split_dot(z * z, gsum)
    inv = lax.rsqrt(ss * (1.0 / HEAD_DIM) + RMS_EPS)
    return z * _split_dot(inv, gexp) * g_row


def _group_mats():
    head_of_lane = jnp.arange(D_MODEL) // HEAD_DIM
    gsum = (head_of_lane[:, None] == jnp.arange(LANES)[None, :]).astype(BF16)
    return gsum, gsum.T


class _Rows:
    def __init__(self, bsz, t_len, tile, precise=False):
        self.bsz, self.t_len = bsz, t_len
        self.precise = precise
        self.rows = bsz * t_len
        self.tm = min(tile, self.rows)
        self.long = t_len >= self.tm
        if self.long:
            assert t_len % self.tm == 0
        else:
            assert self.tm % t_len == 0
        assert self.rows % self.tm == 0
        self.n_tiles = self.rows // self.tm
        self.tiles_per_seq = t_len // self.tm if self.long else 1

    def row_spec(self, width=D_MODEL):
        return pl.BlockSpec((self.tm, width), lambda i, *_: (i, 0))

    def mod_operand(self, m):
        w = m.shape[-1]
        if self.long:
            tps = self.tiles_per_seq
            return m.reshape(self.bsz, 1, w), pl.BlockSpec((None, 1, w), lambda i, *_: (i // tps, 0, 0))
        return jnp.repeat(m, self.t_len, axis=0), pl.BlockSpec((self.tm, w), lambda i, *_: (i, 0))


def _full_spec(a):
    nd = a.ndim
    return pl.BlockSpec(a.shape, lambda *_: (0,) * nd)


def _adaln_kernel(c_ref, w_ref, b_ref, o_ref):
    c = c_ref[...]
    o_ref[...] = jnp.dot(_silu(c), w_ref[...], precision=lax.Precision.HIGHEST,
                         preferred_element_type=F32) + b_ref[...]


def adaln_all(c, w, b, tn=1024):
    m, k = c.shape
    n = w.shape[1]
    return pl.pallas_call(
        _adaln_kernel,
        out_shape=jax.ShapeDtypeStruct((m, n), F32),
        grid=(n // tn,),
        in_specs=[pl.BlockSpec((m, k), lambda j: (0, 0)), pl.BlockSpec((k, tn), lambda j: (0, j)),
                  pl.BlockSpec((1, tn), lambda j: (0, j))],
        out_specs=pl.BlockSpec((m, tn), lambda j: (0, j)),
        compiler_params=_cparams("parallel"),
        name="adaln",
    )(c, w, b.reshape(1, n))


def _rwkv_proj_kernel(*refs, rows, has_vmix):
    (x_ref, sh_ref, sc_ref, first_ref, g_ref, mu_ref, w0_ref, a0_ref,
     wr_ref, wk_ref, wv_ref, w1_ref, w2_ref, a1_ref, a2_ref, g1_ref, g2_ref) = refs[:17]
    pos = 17
    if has_vmix:
        vf_ref, v0_ref, v1_ref, v2_ref = refs[pos:pos + 4]
        pos += 4
    r_ref, w_ref, k_ref, v_ref, a_ref, gg_ref, last_ref = refs[pos:pos + 7]
    carry = refs[pos + 7]
    tm = rows.tm
    i = pl.program_id(0)

    xn = _modulate(x_ref[...], g_ref[...], sh_ref[...], sc_ref[...])
    rolled = pltpu.roll(xn, 1, 0)
    row = lax.broadcasted_iota(jnp.int32, (tm, 1), 0)
    if rows.long:
        @pl.when(i % rows.tiles_per_seq == 0)
        def _():
            carry[...] = first_ref[...]
        x_prev = jnp.where(row == 0, carry[...], rolled)
        carry[...] = xn[tm - 1:tm, :]
        last_ref[...] = xn[tm - 1:tm, :]
    else:
        x_prev = jnp.where(row % rows.t_len == 0, first_ref[...], rolled)
        last_ref[...] = xn
    xx = x_prev - xn
    mu = mu_ref[...]

    mm = functools.partial(_mm, precise=rows.precise)

    def mix(j):
        z = xn + xx * mu[j:j + 1, :]
        return z if rows.precise else z.astype(BF16)

    xv = mix(3)
    r_ref[...] = mm(mix(0), wr_ref[...])
    w_ref[...] = -_softplus(-(w0_ref[...] + mm(jnp.tanh(mm(mix(1), w1_ref[...])), w2_ref[...]))) - 0.5
    k_ref[...] = mm(mix(2), wk_ref[...])
    v = mm(xv, wv_ref[...])
    if has_vmix:
        v = v + (vf_ref[...] - v) * _sigmoid(v0_ref[...] + mm(mm(xv, v1_ref[...]), v2_ref[...]))
    v_ref[...] = v
    a_ref[...] = _sigmoid(a0_ref[...] + mm(mm(mix(4), a1_ref[...]), a2_ref[...]))
    gg_ref[...] = mm(_sigmoid(mm(mix(5), g1_ref[...])), g2_ref[...])


def rwkv_proj(rows, x, sh, sc, shift0, norm_g, mu, w0, a0, wr, wk, wv, w1, w2, a1, a2, g1, g2, vmix):
    d = D_MODEL
    sh_op, sh_spec = rows.mod_operand(sh)
    sc_op, sc_spec = rows.mod_operand(sc)
    first_op, first_spec = rows.mod_operand(shift0)
    row2 = lambda z: z.reshape(1, -1)
    wcast = (lambda z: z) if rows.precise else (lambda z: z.astype(BF16))
    ops = [x, sh_op, sc_op, first_op, row2(norm_g), mu, row2(w0), row2(a0)] + [
        wcast(z) for z in (wr, wk, wv, w1, w2, a1, a2, g1, g2)]
    specs = [rows.row_spec(), sh_spec, sc_spec, first_spec] + [_full_spec(o) for o in ops[4:]]
    if vmix is not None:
        v_first, v0, v1, v2 = vmix
        extra = [v_first, row2(v0), wcast(v1), wcast(v2)]
        ops += extra
        specs += [rows.row_spec()] + [_full_spec(o) for o in extra[1:]]
    act = jax.ShapeDtypeStruct((rows.rows, d), F32)
    if rows.long:
        tps = rows.tiles_per_seq
        last_shape = jax.ShapeDtypeStruct((rows.bsz, 1, d), F32)
        last_spec = pl.BlockSpec((None, 1, d), lambda i: (i // tps, 0, 0))
    else:
        last_shape, last_spec = act, rows.row_spec()
    outs = pl.pallas_call(
        functools.partial(_rwkv_proj_kernel, rows=rows, has_vmix=vmix is not None),
        out_shape=(act,) * 6 + (last_shape,),
        grid=(rows.n_tiles,),
        in_specs=specs,
        out_specs=(rows.row_spec(),) * 6 + (last_spec,),
        scratch_shapes=[pltpu.VMEM((1, d), F32)],
        compiler_params=_cparams("arbitrary"),
        name="rwkv_proj",
    )(*ops)
    last = outs[6]
    last = last.reshape(rows.bsz, d) if rows.long else last.reshape(rows.bsz, rows.t_len, d)[:, -1]
    return outs[:6] + (last,)


def _wkv_kernel(r_ref, w_ref, k_ref, v_ref, a_ref, kk_ref, ka_ref, rk_ref, lw_ref, lb_ref, s0_ref,
                y_ref, st_ref, S, nk_s, bb_s, km_s, dc_s, y_s, *, tt):
    tb = pl.program_id(1)

    @pl.when(tb == 0)
    def _():
        S[...] = s0_ref[...]

    k = k_ref[...]
    a = a_ref[...]
    kkr = k * kk_ref[...][None]
    ss = jnp.sum(kkr * kkr, axis=1, keepdims=True)
    kk = kkr * lax.rsqrt(jnp.maximum(ss, 1e-24))
    km = k * (1.0 + (a - 1.0) * ka_ref[...][None])
    nk_s[...] = -kk
    bb_s[...] = kk * a
    km_s[...] = km
    dc_s[...] = jnp.exp(-jnp.exp(w_ref[...]))

    def step(t, carry):
        def row(ref, j):
            return jnp.broadcast_to(ref[t, pl.ds(j, 1), :], (HEAD_DIM, LANES))

        sa = S[0] * row(nk_s, 0)
        for j in range(1, HEAD_DIM):
            sa = sa + S[j] * row(nk_s, j)
        vt = v_ref[t]
        y = None
        for j in range(HEAD_DIM):
            s_new = S[j] * row(dc_s, j) + sa * row(bb_s, j) + vt * row(km_s, j)
            S[j] = s_new
            yj = s_new * row(r_ref, j)
            y = yj if y is None else y + yj
        y_s[t] = y
        return carry

    lax.fori_loop(0, tt, step, 0)

    y = y_s[...]
    mean = jnp.mean(y, axis=1, keepdims=True)
    yc = y - mean
    var = jnp.mean(yc * yc, axis=1, keepdims=True)
    bonus = jnp.sum(r_ref[...] * km * rk_ref[...][None], axis=1, keepdims=True) * v_ref[...]
    y_ref[...] = yc * lax.rsqrt(var + GN_EPS) * lw_ref[...][None] + lb_ref[...][None] + bonus

    @pl.when(tb == pl.num_programs(1) - 1)
    def _():
        st_ref[...] = S[...]


def wkv_scan(r, w, k, v, a, kk_p, ka_p, rk_p, lw_p, lb_p, s0, *, tt=16):
    t_len, n, l_tot = r.shape
    tt = min(tt, t_len)
    assert t_len % tt == 0 and l_tot % LANES == 0
    seq_spec = pl.BlockSpec((tt, n, LANES), lambda g, t: (t, 0, g))
    par_spec = pl.BlockSpec((n, LANES), lambda g, t: (0, g))
    st_spec = pl.BlockSpec((n, n, LANES), lambda g, t: (0, 0, g))
    return pl.pallas_call(
        functools.partial(_wkv_kernel, tt=tt),
        out_shape=(jax.ShapeDtypeStruct(r.shape, F32), jax.ShapeDtypeStruct(s0.shape, F32)),
        grid=(l_tot // LANES, t_len // tt),
        in_specs=[seq_spec] * 5 + [par_spec] * 5 + [st_spec],
        out_specs=(seq_spec, st_spec),
        scratch_shapes=[pltpu.VMEM((n, n, LANES), F32)] + [pltpu.VMEM((tt, n, LANES), F32)] * 5,
        compiler_params=_cparams("parallel", "arbitrary"),
        name="wkv_scan",
    )(r, w, k, v, a, kk_p, ka_p, rk_p, lw_p, lb_p, s0)


LANE_BATCH = LANES // N_HEADS
CHANNEL_ORDER = tuple(int(h * HEAD_DIM + n) for n in range(HEAD_DIM) for h in range(N_HEADS))


def _wkv_rows_kernel(r_ref, w_ref, k_ref, v_ref, a_ref, kk_ref, ka_ref, rk_ref, lw_ref, lb_ref, s0_ref,
                     y_ref, st_ref, S, r_s, dc_s, km_s, v_s, bb_s, nk_s, y_s, *, tt):
    tb = pl.program_id(1)
    n = HEAD_DIM
    n_chunks = D_MODEL // LANES
    per_chunk = LANES // N_HEADS

    @pl.when(tb == 0)
    def _():
        S[...] = s0_ref[...]

    def to_lanes(src, dst, time_major):
        for c in range(n_chunks):
            xt = [src[b, :, c * LANES:(c + 1) * LANES].T for b in range(LANE_BATCH)]
            for i in range(per_chunk):
                m = jnp.concatenate([x[i * N_HEADS:(i + 1) * N_HEADS, :] for x in xt], axis=0)
                ch = c * per_chunk + i
                if time_major:
                    dst[pl.ds(ch, tt, stride=n), :] = m.T
                else:
                    dst[pl.ds(ch * tt, tt), :] = m.T

    def from_lanes(src, dst):
        for c in range(n_chunks):
            parts = [src[pl.ds(c * per_chunk + i, tt, stride=n), :].T for i in range(per_chunk)]
            for b in range(LANE_BATCH):
                xb = jnp.concatenate([p[b * N_HEADS:(b + 1) * N_HEADS, :] for p in parts], axis=0)
                dst[b, :, c * LANES:(c + 1) * LANES] = xb.T

    to_lanes(r_ref, r_s, False)
    to_lanes(w_ref, dc_s, False)
    to_lanes(k_ref, km_s, False)
    to_lanes(v_ref, v_s, True)
    to_lanes(a_ref, bb_s, False)

    def by_channel(z):
        return z.reshape(n, tt, LANES)

    def per_channel(p_ref):
        return p_ref[...][:, None, :]

    k = by_channel(km_s[...])
    a = by_channel(bb_s[...])
    kkr = k * per_channel(kk_ref)
    ss = jnp.sum(kkr * kkr, axis=0, keepdims=True)
    kk = kkr * lax.rsqrt(jnp.maximum(ss, 1e-24))
    km = k * (1.0 + (a - 1.0) * per_channel(ka_ref))
    rkk = jnp.sum(by_channel(r_s[...]) * km * per_channel(rk_ref), axis=0)
    nk_s[pl.ds(0, tt * n), :] = (-kk).reshape(n * tt, LANES)
    bb_s[...] = (kk * a).reshape(n * tt, LANES)
    km_s[...] = km.reshape(n * tt, LANES)
    dc_s[...] = jnp.exp(-jnp.exp(dc_s[...]))

    def row(ref, t, j):
        return jnp.broadcast_to(ref[pl.ds(j * tt + t, 1), :], (n, LANES))

    nk_s[pl.ds(tt * n, 8), :] = jnp.zeros((8, LANES), F32)
    sa0 = S[0] * row(nk_s, 0, 0)
    for j in range(1, n):
        sa0 = sa0 + S[j] * row(nk_s, 0, j)

    def step(t, sa):
        base = pl.multiple_of(t * n, n)
        vt = v_s[pl.ds(base, n), :]
        y = None
        sa_next = None
        for j in range(n):
            s_new = S[j] * row(dc_s, t, j) + sa * row(bb_s, t, j) + vt * row(km_s, t, j)
            S[j] = s_new
            yj = s_new * row(r_s, t, j)
            y = yj if y is None else y + yj
            sj = s_new * row(nk_s, t + 1, j)
            sa_next = sj if sa_next is None else sa_next + sj
        y_s[pl.ds(base, n), :] = y
        return sa_next

    lax.fori_loop(0, tt, step, sa0)

    y = y_s[...].reshape(tt, n, LANES)
    mean = jnp.mean(y, axis=1, keepdims=True)
    yc = y - mean
    var = jnp.mean(yc * yc, axis=1, keepdims=True)
    bonus = rkk[:, None, :] * v_s[...].reshape(tt, n, LANES)
    y_s[...] = (yc * lax.rsqrt(var + GN_EPS) * lw_ref[...][None] + lb_ref[...][None] + bonus).reshape(
        tt * n, LANES)
    from_lanes(y_s, y_ref)

    @pl.when(tb == pl.num_programs(1) - 1)
    def _():
        st_ref[...] = S[...]


def wkv_scan_rows(r, w, k, v, a, kk_p, ka_p, rk_p, lw_p, lb_p, s0, *, tt=64):
    bsz, t_len, d = r.shape
    n = HEAD_DIM
    assert bsz % LANE_BATCH == 0 and t_len % tt == 0
    seq_spec = pl.BlockSpec((LANE_BATCH, tt, d), lambda g, t: (g, t, 0))
    par_spec = pl.BlockSpec((n, LANES), lambda g, t: (0, g))
    st_spec = pl.BlockSpec((n, n, LANES), lambda g, t: (0, 0, g))
    return pl.pallas_call(
        functools.partial(_wkv_rows_kernel, tt=tt),
        out_shape=(jax.ShapeDtypeStruct(r.shape, F32), jax.ShapeDtypeStruct(s0.shape, F32)),
        grid=(bsz // LANE_BATCH, t_len // tt),
        in_specs=[seq_spec] * 5 + [par_spec] * 5 + [st_spec],
        out_specs=(seq_spec, st_spec),
        scratch_shapes=[pltpu.VMEM((n, n, LANES), F32)] + [pltpu.VMEM((tt * n, LANES), F32)] * 5
        + [pltpu.VMEM(((tt + 1) * n, LANES), F32), pltpu.VMEM((tt * n, LANES), F32)],
        compiler_params=_cparams("parallel", "arbitrary"),
        name="wkv_scan_rows",
    )(r, w, k, v, a, kk_p, ka_p, rk_p, lw_p, lb_p, s0)


def _to_lanes(z, bsz, t_len):
    return jnp.transpose(z.reshape(bsz, t_len, N_HEADS, HEAD_DIM), (1, 3, 0, 2)).reshape(
        t_len, HEAD_DIM, bsz * N_HEADS)


def _from_lanes(z, bsz, t_len):
    return jnp.transpose(z.reshape(t_len, HEAD_DIM, bsz, N_HEADS), (2, 0, 3, 1)).reshape(
        bsz * t_len, D_MODEL)


def _param_lanes(p, bsz):
    return jnp.broadcast_to(p.reshape(1, N_HEADS, HEAD_DIM), (bsz, N_HEADS, HEAD_DIM)).transpose(
        2, 0, 1).reshape(HEAD_DIM, bsz * N_HEADS)


def _resid_mm_kernel(*refs, has_mul, precise):
    if has_mul:
        a_ref, b_ref, w_ref, x_ref, gt_ref, o_ref = refs
        lhs = a_ref[...] * b_ref[...]
    else:
        a_ref, w_ref, x_ref, gt_ref, o_ref = refs
        lhs = a_ref[...]
    o_ref[...] = x_ref[...] + gt_ref[...] * _mm(lhs, w_ref[...], precise)


def resid_mm(rows, a, b, w, x, gate):
    gt_op, gt_spec = rows.mod_operand(gate)
    w = w if rows.precise else w.astype(BF16)
    ops = [a] + ([b] if b is not None else []) + [w, x, gt_op]
    specs = [rows.row_spec()] * (2 if b is not None else 1) + [_full_spec(w), rows.row_spec(), gt_spec]
    return pl.pallas_call(
        functools.partial(_resid_mm_kernel, has_mul=b is not None, precise=rows.precise),
        out_shape=jax.ShapeDtypeStruct(x.shape, F32),
        grid=(rows.n_tiles,),
        in_specs=specs,
        out_specs=rows.row_spec(),
        compiler_params=_cparams("parallel"),
        name="resid_mm",
    )(*ops)


def _route(probs, bias):
    tm = probs.shape[0]
    sel = probs + bias
    lane = lax.broadcasted_iota(jnp.int32, (tm, N_EXPERTS), 1)
    grp = lane // EXPERTS_PER_GROUP

    def top2(mask):
        x = jnp.where(mask, sel, NEG_INF)
        v1 = jnp.max(x, axis=-1, keepdims=True)
        i1 = jnp.min(jnp.where(x == v1, lane, N_EXPERTS), axis=-1, keepdims=True)
        x2 = jnp.where(lane == i1, NEG_INF, x)
        v2 = jnp.max(x2, axis=-1, keepdims=True)
        i2 = jnp.min(jnp.where(x2 == v2, lane, N_EXPERTS), axis=-1, keepdims=True)
        return v1 + v2, i1, i2

    best_score, _, _ = top2(grp == 0)
    best = jnp.zeros((tm, 1), jnp.int32)
    for g in range(1, N_GROUPS):
        score, _, _ = top2(grp == g)
        better = score > best_score
        best = jnp.where(better, g, best)
        best_score = jnp.where(better, score, best_score)
    _, i1, i2 = top2(grp == best)
    chosen = (lane == i1) | (lane == i2)
    w = jnp.where(chosen, probs, 0.0)
    return w / jnp.sum(w, axis=-1, keepdims=True), best


def _router_gate(xn, rw, rb):
    logits = jnp.dot(xn, rw, precision=lax.Precision.HIGHEST, preferred_element_type=F32)
    z = jnp.exp(logits - jnp.max(logits, axis=-1, keepdims=True))
    return _route(z / jnp.sum(z, axis=-1, keepdims=True), rb)


def _moe_kernel(x_ref, sh_ref, sc_ref, gt_ref, g_ref, rw_ref, rb_ref, wg_ref, wu_ref, wd_ref,
                o_ref, xn_s, gate_s, acc_s, *, precise):
    e = pl.program_id(1)

    @pl.when(e == 0)
    def _():
        xn = _modulate(x_ref[...], g_ref[...], sh_ref[...], sc_ref[...])
        xn_s[...] = xn.astype(xn_s.dtype)
        gate_s[...], _ = _router_gate(xn, rw_ref[...], rb_ref[...])
        acc_s[...] = jnp.zeros_like(acc_s)

    acc_s[...] += _expert_ffn(xn_s[...], gate_s[...], e, wg_ref, wu_ref, wd_ref, precise)

    @pl.when(e == pl.num_programs(1) - 1)
    def _():
        o_ref[...] = x_ref[...] + gt_ref[...] * acc_s[...]


def _expert_ffn(xn, gate, expert, wg_ref, wu_ref, wd_ref, precise=False):
    lane = lax.broadcasted_iota(jnp.int32, gate.shape, 1)
    g_e = jnp.sum(jnp.where(lane == expert, gate, 0.0), axis=-1, keepdims=True)
    h = _silu(_mm(xn, wg_ref[...], precise)) * _mm(xn, wu_ref[...], precise)
    return _mm(h * g_e, wd_ref[...], precise)


def moe_layer(rows, x, sh, sc, gate, norm_g, router_w, router_bias, w_gate, w_up, w_down, layer):
    d = D_MODEL
    sh_op, sh_spec = rows.mod_operand(sh)
    sc_op, sc_spec = rows.mod_operand(sc)
    gt_op, gt_spec = rows.mod_operand(gate)
    g2 = norm_g.reshape(1, d)
    rb = router_bias.reshape(1, N_EXPERTS)
    tm = rows.tm
    return pl.pallas_call(
        functools.partial(_moe_kernel, precise=rows.precise),
        out_shape=jax.ShapeDtypeStruct(x.shape, F32),
        grid=(rows.n_tiles, N_EXPERTS),
        in_specs=[rows.row_spec(), sh_spec, sc_spec, gt_spec, _full_spec(g2), _full_spec(router_w),
                  _full_spec(rb),
                  pl.BlockSpec((None, None, d, D_EXPERT), lambda i, e: (layer, e, 0, 0)),
                  pl.BlockSpec((None, None, d, D_EXPERT), lambda i, e: (layer, e, 0, 0)),
                  pl.BlockSpec((None, None, D_EXPERT, d), lambda i, e: (layer, e, 0, 0))],
        out_specs=rows.row_spec(),
        scratch_shapes=[pltpu.VMEM((tm, d), F32 if rows.precise else BF16),
                        pltpu.VMEM((tm, N_EXPERTS), F32), pltpu.VMEM((tm, d), F32)],
        compiler_params=_cparams("parallel", "arbitrary"),
        name="moe",
    )(x, sh_op, sc_op, gt_op, g2, router_w, rb, w_gate, w_up, w_down)


PAYLOAD = D_MODEL + LANES


def _moe_route_kernel(x_ref, sh_ref, sc_ref, g_ref, rw_ref, rb_ref, xg_ref, grp_ref):
    xn = _modulate(x_ref[...], g_ref[...], sh_ref[...], sc_ref[...])
    gate, best = _router_gate(xn, rw_ref[...], rb_ref[...])
    xg_ref[:, :D_MODEL] = xn
    xg_ref[:, D_MODEL:] = jnp.zeros((xn.shape[0], LANES), F32)
    xg_ref[:, D_MODEL:D_MODEL + N_EXPERTS] = gate
    grp_ref[...] = best


def moe_route(rows, x, sh, sc, norm_g, router_w, router_bias):
    d = D_MODEL
    sh_op, sh_spec = rows.mod_operand(sh)
    sc_op, sc_spec = rows.mod_operand(sc)
    g2 = norm_g.reshape(1, d)
    rb = router_bias.reshape(1, N_EXPERTS)
    return pl.pallas_call(
        _moe_route_kernel,
        out_shape=(jax.ShapeDtypeStruct((rows.rows, PAYLOAD), F32),
                   jax.ShapeDtypeStruct((rows.rows, 1), jnp.int32)),
        grid=(rows.n_tiles,),
        in_specs=[rows.row_spec(), sh_spec, sc_spec, _full_spec(g2), _full_spec(router_w), _full_spec(rb)],
        out_specs=(rows.row_spec(PAYLOAD), rows.row_spec(1)),
        compiler_params=_cparams("parallel"),
        name="moe_route",
    )(x, sh_op, sc_op, g2, router_w, rb)


def _group_order(grp, tm):
    n = grp.shape[0]
    p_slots = n + N_GROUPS * tm
    onehot = (grp[:, None] == jnp.arange(N_GROUPS, dtype=jnp.int32)[None, :]).astype(jnp.int32)
    counts = jnp.sum(onehot, axis=0)
    rank = jnp.sum((jnp.cumsum(onehot, axis=0) - onehot) * onehot, axis=1)
    padded = ((counts + tm - 1) // tm) * tm
    ends = jnp.cumsum(padded)
    pos = jnp.sum(onehot * (ends - padded)[None, :], axis=1) + rank
    token = jnp.arange(n, dtype=jnp.int32)
    filled = jnp.zeros((p_slots + 2 * tm,), jnp.int32).at[pos].set(token + 1)
    src = jnp.maximum(filled - 1, 0)
    real = jnp.minimum(filled[:p_slots], 1)
    dst = jnp.where(real == 1, src[:p_slots], n + jnp.cumsum(1 - real) - 1).astype(jnp.int32)
    dst_prev = jnp.concatenate([p_slots + jnp.arange(tm, dtype=jnp.int32), dst])
    tile_start = jnp.arange(p_slots // tm + 1, dtype=jnp.int32) * tm
    tile_grp = jnp.sum((tile_start[:, None] >= ends[None, :]).astype(jnp.int32), axis=1)
    tile_grp = jnp.where(tile_grp < N_GROUPS, tile_grp, -1).astype(jnp.int32)
    return src, dst_prev, tile_grp


def _moe_sparse_kernel(src_ref, dst_ref, tg_ref, xg_hbm, wg_ref, wu_ref, wd_ref, y_hbm,
                       xbuf, ybuf, xn_s, gate_s, acc_s, gsem, ssem, *, tm):
    i = pl.program_id(0)
    e = pl.program_id(1)
    nt = pl.num_programs(0)
    slot = i % 2
    prev = 1 - slot
    quarter = tm // EXPERTS_PER_GROUP

    def row_in(tile, to_slot, k):
        return pltpu.make_async_copy(xg_hbm.at[pl.ds(src_ref[tile * tm + k], 1), :],
                                     xbuf.at[to_slot, pl.ds(k, 1), :], gsem.at[to_slot])

    def row_out(tile, from_slot, k):
        return pltpu.make_async_copy(ybuf.at[from_slot, pl.ds(k, 1), :],
                                     y_hbm.at[pl.ds(dst_ref[tile * tm + k], 1), :], ssem.at[from_slot])

    def wait_in(slot_):
        pltpu.make_async_copy(xg_hbm.at[pl.ds(0, tm), :], xbuf.at[slot_], gsem.at[slot_]).wait()

    def wait_out(slot_):
        pltpu.make_async_copy(ybuf.at[slot_], y_hbm.at[pl.ds(0, tm), :], ssem.at[slot_]).wait()

    @pl.when((i == 0) & (e == 0))
    def _():
        def body(k, c):
            row_in(0, 0, k).start()
            return c
        lax.fori_loop(0, tm, body, 0)
        ybuf[1] = jnp.zeros((tm, D_MODEL), F32)

    @pl.when(e == 0)
    def _():
        wait_in(slot)
        xn_s[...] = xbuf[slot, :, :D_MODEL].astype(BF16)
        gate_s[...] = xbuf[slot, :, D_MODEL:]

    def start_neighbour_rows():
        for kk in range(quarter):
            k = e * quarter + kk
            row_in(i + 1, prev, k).start()
            row_out(i, prev, k).start()

    grp = tg_ref[i]

    @pl.when(grp >= 0)
    def _():
        start_neighbour_rows()
        contrib = _expert_ffn(xn_s[...], gate_s[...], grp * EXPERTS_PER_GROUP + e, wg_ref, wu_ref, wd_ref)

        @pl.when(e == 0)
        def _():
            acc_s[...] = contrib

        @pl.when(e > 0)
        def _():
            acc_s[...] += contrib

    @pl.when(grp < 0)
    def _():
        start_neighbour_rows()

    @pl.when(e == EXPERTS_PER_GROUP - 1)
    def _():
        @pl.when(i >= 1)
        def _():
            wait_out(slot)

        @pl.when(grp >= 0)
        def _():
            ybuf[slot] = acc_s[...]

        @pl.when(i == nt - 1)
        def _():
            wait_out(prev)
            wait_in(prev)


def moe_sparse(xg, grp, w_gate, w_up, w_down, layer, tm=512):
    d = D_MODEL
    src, dst_prev, tile_grp = _group_order(grp, tm)
    n_tiles = tile_grp.shape[0]
    expert_of = lambda i, e, s, t, tg: (layer, jnp.maximum(tg[i], 0) * EXPERTS_PER_GROUP + e, 0, 0)
    return pl.pallas_call(
        functools.partial(_moe_sparse_kernel, tm=tm),
        out_shape=jax.ShapeDtypeStruct((n_tiles * tm, d), F32),
        grid_spec=pltpu.PrefetchScalarGridSpec(
            num_scalar_prefetch=3,
            grid=(n_tiles, EXPERTS_PER_GROUP),
            in_specs=[pl.BlockSpec(memory_space=pl.ANY),
                      pl.BlockSpec((None, None, d, D_EXPERT), expert_of),
                      pl.BlockSpec((None, None, d, D_EXPERT), expert_of),
                      pl.BlockSpec((None, None, D_EXPERT, d), expert_of)],
            out_specs=pl.BlockSpec(memory_space=pl.ANY),
            scratch_shapes=[pltpu.VMEM((2, tm, PAYLOAD), F32), pltpu.VMEM((2, tm, d), F32),
                            pltpu.VMEM((tm, d), BF16), pltpu.VMEM((tm, LANES), F32),
                            pltpu.VMEM((tm, d), F32),
                            pltpu.SemaphoreType.DMA((2,)), pltpu.SemaphoreType.DMA((2,))]),
        compiler_params=_cparams("arbitrary", "arbitrary"),
        name="moe_sparse",
    )(src, dst_prev, tile_grp, xg, w_gate, w_up, w_down)


def _resid_add_kernel(x_ref, y_ref, gt_ref, o_ref):
    o_ref[...] = x_ref[...] + gt_ref[...] * y_ref[...]


def resid_add(rows, x, y, gate):
    gt_op, gt_spec = rows.mod_operand(gate)
    return pl.pallas_call(
        _resid_add_kernel,
        out_shape=jax.ShapeDtypeStruct(x.shape, F32),
        grid=(rows.n_tiles,),
        in_specs=[rows.row_spec(), rows.row_spec(), gt_spec],
        out_specs=rows.row_spec(),
        compiler_params=_cparams("parallel"),
        name="resid_add",
    )(x, y, gt_op)


def _kv_proj_kernel(x_ref, sh_ref, sc_ref, g_ref, wk_ref, wv_ref, wf_ref, bf_ref, kg_ref, gsum_ref,
                    gexp_ref, k_ref, v_ref, lf_ref, fc_ref, carry, *, rows):
    tm = rows.tm
    i = pl.program_id(0)
    xn = _modulate(x_ref[...], g_ref[...], sh_ref[...], sc_ref[...]).astype(BF16)
    k_ref[...] = _head_rms(jnp.dot(xn, wk_ref[...], preferred_element_type=F32), gsum_ref[...],
                           gexp_ref[...], kg_ref[...])
    v_ref[...] = jnp.dot(xn, wv_ref[...], preferred_element_type=F32)
    z = jnp.dot(xn, wf_ref[...], preferred_element_type=F32) + bf_ref[...]
    lf = -_softplus(-z)
    lf_ref[...] = lf[:, :N_HEADS]
    ri = lax.broadcasted_iota(jnp.int32, (tm, tm), 0)
    ci = lax.broadcasted_iota(jnp.int32, (tm, tm), 1)
    if rows.long:
        tri = (ci <= ri).astype(BF16)

        @pl.when(i % rows.tiles_per_seq == 0)
        def _():
            carry[...] = jnp.zeros_like(carry)
        cum = _split_dot_lhs_exact(tri, lf) + carry[...]
        carry[...] = cum[tm - 1:tm, :]
    else:
        tri = ((ci <= ri) & (ci // rows.t_len == ri // rows.t_len)).astype(BF16)
        cum = _split_dot_lhs_exact(tri, lf)
    fc_ref[...] = cum[:, :N_HEADS]


def _split_dot_lhs_exact(m, b):
    b1 = b.astype(BF16)
    r1 = b - b1.astype(F32)
    b2 = r1.astype(BF16)
    b3 = (r1 - b2.astype(F32)).astype(BF16)
    return (jnp.dot(m, b1, preferred_element_type=F32) + jnp.dot(m, b2, preferred_element_type=F32)
            + jnp.dot(m, b3, preferred_element_type=F32))


def kv_proj(rows, x, sh, sc, norm_g, kv_w, kv_b_f, k_norm_g):
    d = D_MODEL
    sh_op, sh_spec = rows.mod_operand(sh)
    sc_op, sc_spec = rows.mod_operand(sc)
    gsum, gexp = _group_mats()
    wk = kv_w[:, :d].astype(BF16)
    wv = kv_w[:, d:2 * d].astype(BF16)
    wf = jnp.pad(kv_w[:, 2 * d:], ((0, 0), (0, LANES - N_HEADS))).astype(BF16)
    bf = jnp.pad(kv_b_f, (0, LANES - N_HEADS)).reshape(1, LANES)
    kg = jnp.tile(k_norm_g, N_HEADS).reshape(1, d)
    consts = [norm_g.reshape(1, d), wk, wv, wf, bf, kg, gsum, gexp]
    act = jax.ShapeDtypeStruct((rows.rows, d), F32)
    small = jax.ShapeDtypeStruct((rows.rows, N_HEADS), F32)
    return pl.pallas_call(
        functools.partial(_kv_proj_kernel, rows=rows),
        out_shape=(act, act, small, small),
        grid=(rows.n_tiles,),
        in_specs=[rows.row_spec(), sh_spec, sc_spec] + [_full_spec(c) for c in consts],
        out_specs=(rows.row_spec(), rows.row_spec(), rows.row_spec(N_HEADS), rows.row_spec(N_HEADS)),
        scratch_shapes=[pltpu.VMEM((1, LANES), F32)],
        compiler_params=_cparams("arbitrary"),
        name="kv_proj",
    )(x, sh_op, sc_op, *consts)


def _q_proj_kernel(x_ref, sh_ref, sc_ref, g_ref, wq_ref, qg_ref, gsum_ref, gexp_ref, q_ref):
    xn = _modulate(x_ref[...], g_ref[...], sh_ref[...], sc_ref[...])
    q = _head_rms(_bdot(xn, wq_ref[...]), gsum_ref[...], gexp_ref[...], qg_ref[...])
    q_ref[...] = q * (HEAD_DIM ** -0.5 * LOG2E)


def q_proj(rows, x, sh, sc, norm_g, wq, q_norm_g):
    d = D_MODEL
    sh_op, sh_spec = rows.mod_operand(sh)
    sc_op, sc_spec = rows.mod_operand(sc)
    gsum, gexp = _group_mats()
    consts = [norm_g.reshape(1, d), wq.astype(BF16), jnp.tile(q_norm_g, N_HEADS).reshape(1, d), gsum, gexp]
    return pl.pallas_call(
        _q_proj_kernel,
        out_shape=jax.ShapeDtypeStruct((rows.rows, d), F32),
        grid=(rows.n_tiles,),
        in_specs=[rows.row_spec(), sh_spec, sc_spec] + [_full_spec(c) for c in consts],
        out_specs=rows.row_spec(),
        compiler_params=_cparams("parallel"),
        name="q_proj",
    )(x, sh_op, sc_op, *consts)


def _fox_prompt_kernel(q_ref, k_ref, v_ref, f_ref, o_ref, m_s, l_s, acc_s, *, tile):
    hp = pl.program_id(1)
    qi = pl.program_id(2)
    lane = lax.broadcasted_iota(jnp.int32, (1, LANES), 1)
    q = q_ref[...]
    qms = []
    for hh in range(2):
        qms.append(jnp.where((lane // HEAD_DIM) == hh, q, 0.0).astype(BF16))
        m_s[hh] = jnp.full((1, tile), NEG_INF, F32)
        l_s[hh] = jnp.zeros((1, tile), F32)
        acc_s[hh] = jnp.zeros((LANES, tile), F32)

    def step(ki, diagonal):
        start = pl.multiple_of(ki * tile, tile)
        kt = k_ref[pl.ds(start, tile), :].astype(BF16)
        vt = v_ref[pl.ds(start, tile), :].astype(BF16)
        fk = f_ref[pl.ds(start, tile), :]
        lane_h = lax.broadcasted_iota(jnp.int32, fk.shape, 1)
        scores = []
        for hh in range(2):
            f_key = jnp.sum(jnp.where(lane_h == 2 * hp + hh, fk, 0.0), axis=-1, keepdims=True) * LOG2E
            s = lax.dot_general(kt, qms[hh], (((1,), (1,)), ((), ())), preferred_element_type=F32) - f_key
            if diagonal:
                key = lax.broadcasted_iota(jnp.int32, (tile, tile), 0)
                qry = lax.broadcasted_iota(jnp.int32, (tile, tile), 1)
                s = jnp.where(key <= qry, s, NEG_INF)
            scores.append(s)
        probs = []
        for hh in range(2):
            m_old = m_s[hh]
            m_new = jnp.maximum(m_old, jnp.max(scores[hh], axis=0, keepdims=True))
            p = jnp.exp2(scores[hh] - m_new)
            alpha = jnp.exp2(m_old - m_new)
            l_s[hh] = alpha * l_s[hh] + jnp.sum(p, axis=0, keepdims=True)
            m_s[hh] = m_new
            probs.append((alpha, p.astype(BF16)))
        for hh in range(2):
            alpha, p = probs[hh]
            acc_s[hh] = alpha * acc_s[hh] + lax.dot_general(
                vt, p, (((0,), (0,)), ((), ())), preferred_element_type=F32)

    def body(ki, c):
        step(ki, False)
        return c

    lax.fori_loop(0, qi, body, 0)
    step(qi, True)
    chan = lax.broadcasted_iota(jnp.int32, (LANES, 1), 0)
    o_t = jnp.where((chan // HEAD_DIM) == 0, acc_s[0] / l_s[0], acc_s[1] / l_s[1])
    o_ref[...] = o_t.T


def fox_prompt_attention(q, k, v, fcum, bsz, t_len, tile=512):
    d = D_MODEL
    tile = min(tile, t_len)
    q3, k3, v3 = (z.reshape(bsz, t_len, d) for z in (q, k, v))
    f3 = fcum.reshape(bsz, t_len, N_HEADS)
    out = pl.pallas_call(
        functools.partial(_fox_prompt_kernel, tile=tile),
        out_shape=jax.ShapeDtypeStruct((bsz, t_len, d), F32),
        grid=(bsz, N_HEADS // 2, t_len // tile),
        in_specs=[pl.BlockSpec((None, tile, LANES), lambda b, h, i: (b, i, h)),
                  pl.BlockSpec((None, t_len, LANES), lambda b, h, i: (b, 0, h)),
                  pl.BlockSpec((None, t_len, LANES), lambda b, h, i: (b, 0, h)),
                  pl.BlockSpec((None, t_len, N_HEADS), lambda b, h, i: (b, 0, 0))],
        out_specs=pl.BlockSpec((None, tile, LANES), lambda b, h, i: (b, i, h)),
        scratch_shapes=[pltpu.VMEM((2, 1, tile), F32), pltpu.VMEM((2, 1, tile), F32),
                        pltpu.VMEM((2, LANES, tile), F32)],
        compiler_params=_cparams("parallel", "parallel", "arbitrary"),
        name="fox_prompt",
    )(q3, k3, v3, f3)
    return out.reshape(bsz * t_len, d)


def _fox_sample_kernel(pt_ref, q_ref, kn_ref, vn_ref, fnr_ref, *rest, t_len, page, pps):
    page_refs = rest[:3 * pps]
    o_ref, qbd_s, m_s, l_s, acc_s, rc_s = rest[3 * pps:]
    p = pl.program_id(1)
    nrow = t_len * N_HEADS
    head_of_row = lax.broadcasted_iota(jnp.int32, (nrow, D_MODEL), 0) % N_HEADS
    head_of_col = lax.broadcasted_iota(jnp.int32, (nrow, D_MODEL), 1) // HEAD_DIM
    diag = head_of_row == head_of_col

    @pl.when(p == 0)
    def _():
        q = q_ref[...]
        qrep = jnp.broadcast_to(q[:, None, :], (t_len, N_HEADS, D_MODEL)).reshape(nrow, D_MODEL)
        qbd = jnp.where(diag, qrep, 0.0)
        qbd_s[...] = qbd.astype(BF16)
        q_of_row = lax.broadcasted_iota(jnp.int32, (nrow, 1), 0) // N_HEADS
        fnr = fnr_ref[...] * LOG2E
        s_cols = []
        for t in range(t_len):
            s_t = jnp.sum(qbd * kn_ref[pl.ds(t, 1), :], axis=-1, keepdims=True)
            s_cols.append(jnp.where(q_of_row >= t, s_t - fnr[:, t:t + 1], NEG_INF))
        m = functools.reduce(jnp.maximum, s_cols)
        l = jnp.zeros((nrow, 1), F32)
        acc = jnp.zeros((nrow, D_MODEL), F32)
        for t in range(t_len):
            p_t = jnp.exp2(s_cols[t] - m)
            l = l + p_t
            acc = acc + p_t * vn_ref[pl.ds(t, 1), :]
        m_s[...] = m
        l_s[...] = l
        acc_s[...] = acc
        rc_s[...] = jnp.zeros_like(rc_s)

    ji = lax.broadcasted_iota(jnp.int32, (page, page), 0)
    si = lax.broadcasted_iota(jnp.int32, (page, page), 1)
    later = (ji > si).astype(BF16)
    qbd = qbd_s[...]
    rc = rc_s[...]
    s_parts = []
    for i in range(pps):
        k_ref, _, lf_ref = page_refs[3 * i:3 * i + 3]
        lf = lf_ref[...]
        r16 = _split_dot(lf, later) + rc
        rc = rc + jnp.sum(lf, axis=-1, keepdims=True)
        kt = k_ref[...].reshape(D_MODEL, page).astype(BF16)
        s = jnp.dot(qbd, kt, preferred_element_type=F32)
        s_parts.append(s + jnp.concatenate([r16 * LOG2E] * t_len, axis=0))
    rc_s[...] = rc
    m_old = m_s[...]
    m_new = functools.reduce(jnp.maximum, [m_old] + [jnp.max(s, axis=-1, keepdims=True) for s in s_parts])
    alpha = jnp.exp2(m_old - m_new)
    l = alpha * l_s[...]
    acc = alpha * acc_s[...]
    for i, s in enumerate(s_parts):
        pr = jnp.exp2(s - m_new)
        l = l + jnp.sum(pr, axis=-1, keepdims=True)
        vt = page_refs[3 * i + 1][...].reshape(D_MODEL, page).astype(BF16)
        acc = acc + lax.dot_general(pr.astype(BF16), vt, (((1,), (1,)), ((), ())),
                                    preferred_element_type=F32)
    l_s[...] = l
    acc_s[...] = acc
    m_s[...] = m_new

    @pl.when(p == pl.num_programs(1) - 1)
    def _():
        o = jnp.where(diag, acc_s[...] / l_s[...], 0.0)
        o_ref[...] = jnp.sum(o.reshape(t_len, N_HEADS, D_MODEL), axis=1)


def fox_sample_attention(q, k_new, v_new, fcum, cache_k, cache_v, cache_logf, page_table, bsz, t_len, pps=16):
    d = D_MODEL
    n_pool, page = cache_k.shape[:2]
    n_pages = page_table.shape[1]
    pps = min(pps, n_pages)
    assert n_pages % pps == 0
    nrow = t_len * N_HEADS
    q3, kn3, vn3 = (z.reshape(bsz, t_len, d) for z in (q, k_new, v_new))
    fc3 = fcum.reshape(bsz, t_len, N_HEADS)
    fnr = jnp.broadcast_to(jnp.transpose(fc3, (0, 2, 1))[:, None], (bsz, t_len, N_HEADS, t_len)).reshape(
        bsz, nrow, t_len)
    ck = jnp.transpose(cache_k, (0, 2, 3, 1))
    cv = jnp.transpose(cache_v, (0, 2, 3, 1))
    clf = jnp.transpose(cache_logf, (0, 2, 1))
    per_b = lambda shape: pl.BlockSpec((None,) + shape, lambda b, p, pt: (b,) + (0,) * len(shape))

    def paged(shape, i):
        return pl.BlockSpec((None,) + shape,
                            lambda b, p, pt: (pt[b, n_pages - 1 - (p * pps + i)],) + (0,) * len(shape))

    page_specs, page_ops = [], []
    for i in range(pps):
        page_specs += [paged((N_HEADS, HEAD_DIM, page), i), paged((N_HEADS, HEAD_DIM, page), i),
                       paged((N_HEADS, page), i)]
        page_ops += [ck, cv, clf]
    out = pl.pallas_call(
        functools.partial(_fox_sample_kernel, t_len=t_len, page=page, pps=pps),
        out_shape=jax.ShapeDtypeStruct((bsz, t_len, d), F32),
        grid_spec=pltpu.PrefetchScalarGridSpec(
            num_scalar_prefetch=1,
            grid=(bsz, n_pages // pps),
            in_specs=[per_b((t_len, d)), per_b((t_len, d)), per_b((t_len, d)), per_b((nrow, t_len))]
            + page_specs,
            out_specs=per_b((t_len, d)),
            scratch_shapes=[pltpu.VMEM((nrow, d), BF16), pltpu.VMEM((nrow, 1), F32),
                            pltpu.VMEM((nrow, 1), F32), pltpu.VMEM((nrow, d), F32),
                            pltpu.VMEM((N_HEADS, 1), F32)]),
        compiler_params=_cparams("parallel", "arbitrary"),
        name="fox_sample",
    )(page_table, q3, kn3, vn3, fnr, *page_ops)
    return out.reshape(bsz * t_len, d)


def _trunk(x3, mods, kv_mods, shift0, wkv0, attend, p, moe_w, moe_w32, depth=DEPTH):
    bsz, t_len, d = x3.shape
    x = x3.reshape(bsz * t_len, d)
    shifts, states = [], []
    v_first = None
    k_sh = v_sh = lf_sh = fcum = None
    for i in range(depth):
        precise = i < N_A and not _Rows(bsz, t_len, 256).long
        rows = _Rows(bsz, t_len, 256, precise)
        rows_moe = _Rows(bsz, t_len, 512, precise)
        ew = moe_w32 if precise else moe_w
        sh_m, sc_m, gt_m, sh_f, sc_f, gt_f = jnp.split(mods[i], 6, axis=-1)
        if i < N_A:
            in_rows = rows.long and bsz % LANE_BATCH == 0
            order = jnp.asarray(CHANNEL_ORDER, jnp.int32)
            cols = (lambda z: jnp.take(z, order, axis=-1)) if in_rows else (lambda z: z)
            vmix = None if i == 0 else (v_first, cols(p['a_v0'][i - 1]), p['a_v1'][i - 1], cols(p['a_v2'][i - 1]))
            r, w, k, v, a, g, last = rwkv_proj(
                rows, x, sh_m, sc_m, shift0[i], p['norm_g'][i, 0], p['a_mu'][i], cols(p['a_w0'][i]),
                cols(p['a_a0'][i]), cols(p['a_wr'][i]), cols(p['a_wk'][i]), cols(p['a_wv'][i]), p['a_w1'][i],
                cols(p['a_w2'][i]), p['a_a1'][i], cols(p['a_a2'][i]), p['a_g1'][i], cols(p['a_g2'][i]), vmix)
            if i == 0:
                v_first = v
            shifts.append(last)
            pars = [_param_lanes(z, bsz) for z in (p['a_kk'][i], p['a_ka'][i], p['a_rk'][i].reshape(-1),
                                                   p['a_lnx_w'][i], p['a_lnx_b'][i])]
            s0 = jnp.transpose(wkv0[i], (3, 2, 0, 1)).reshape(HEAD_DIM, HEAD_DIM, bsz * N_HEADS)
            if in_rows:
                y, s_t = wkv_scan_rows(*[z.reshape(bsz, t_len, d) for z in (r, w, k, v, a)], *pars, s0)
                x = resid_mm(rows, y.reshape(bsz * t_len, d), g, jnp.take(p['a_wo'][i], order, axis=0), x, gt_m)
            else:
                y, s_t = wkv_scan(*[_to_lanes(z, bsz, t_len) for z in (r, w, k, v, a)], *pars, s0)
                x = resid_mm(rows, _from_lanes(y, bsz, t_len), g, p['a_wo'][i], x, gt_m)
            states.append(jnp.transpose(s_t.reshape(HEAD_DIM, HEAD_DIM, bsz, N_HEADS), (2, 3, 1, 0)))
        else:
            j = i - N_A
            if j == 0:
                k_sh, v_sh, lf_sh, fcum = kv_proj(rows, x, kv_mods[0], kv_mods[1], p['kv_norm_g'], p['kv_w'],
                                                  p['kv_b_f'], p['k_norm_g'])
            q = q_proj(rows, x, sh_m, sc_m, p['norm_g'][i, 0], p['b_wq'][j], p['b_q_norm_g'][j])
            o = attend(q, k_sh, v_sh, fcum)
            x = resid_mm(rows, o, None, p['b_wo'][j], x, gt_m)
        if rows_moe.rows >= SPARSE_MOE_MIN_ROWS:
            xg, grp = moe_route(rows_moe, x, sh_f, sc_f, p['norm_g'][i, 1], p['router_w'], p['router_bias'])
            y = moe_sparse(xg, grp.reshape(-1), ew[0], ew[1], ew[2], i)
            x = resid_add(rows_moe, x, y, gt_f)
        else:
            x = moe_layer(rows_moe, x, sh_f, sc_f, gt_f, p['norm_g'][i, 1], p['router_w'], p['router_bias'],
                          ew[0], ew[1], ew[2], i)
    if depth < DEPTH:
        return x.reshape(bsz, t_len, d), jnp.stack(shifts)
    hd = (bsz, t_len, N_HEADS, HEAD_DIM)
    return (x.reshape(bsz, t_len, d), jnp.stack(shifts), jnp.stack(states), k_sh.reshape(hd),
            v_sh.reshape(hd), lf_sh.reshape(bsz, t_len, N_HEADS))


def kernel(x_prompt, x_sample, c_prompt, c_sample, state_shift, state_wkv, cache_k, cache_v, cache_logf, page_table, mod_w, mod_b, norm_g, a_mu, a_w0, a_w1, a_w2, a_a0, a_a1, a_a2, a_v0, a_v1, a_v2, a_g1, a_g2, a_kk, a_ka, a_rk, a_wr, a_wk, a_wv, a_wo, a_lnx_w, a_lnx_b, kv_mod_w, kv_mod_b, kv_norm_g, kv_w, kv_b_f, k_norm_g, b_wq, b_q_norm_g, b_wo, router_w, router_bias, moe_w_gate, moe_w_up, moe_w_down):
    p = dict(norm_g=norm_g, a_mu=a_mu, a_w0=a_w0, a_w1=a_w1, a_w2=a_w2, a_a0=a_a0, a_a1=a_a1, a_a2=a_a2,
             a_v0=a_v0, a_v1=a_v1, a_v2=a_v2, a_g1=a_g1, a_g2=a_g2, a_kk=a_kk, a_ka=a_ka, a_rk=a_rk,
             a_wr=a_wr, a_wk=a_wk, a_wv=a_wv, a_wo=a_wo, a_lnx_w=a_lnx_w, a_lnx_b=a_lnx_b,
             kv_norm_g=kv_norm_g, kv_w=kv_w, kv_b_f=kv_b_f, k_norm_g=k_norm_g, b_wq=b_wq,
             b_q_norm_g=b_q_norm_g, b_wo=b_wo, router_w=router_w, router_bias=router_bias)
    d = D_MODEL
    bp, t_p = x_prompt.shape[:2]
    bs, t_s = x_sample.shape[:2]
    moe_w = (moe_w_gate.astype(BF16), moe_w_up.astype(BF16), moe_w_down.astype(BF16))

    w_all = jnp.concatenate([mod_w[i] for i in range(DEPTH)] + [kv_mod_w], axis=1)
    b_all = jnp.concatenate([mod_b[i] for i in range(DEPTH)] + [kv_mod_b], axis=0)
    m_all = adaln_all(jnp.concatenate([c_prompt, c_sample], axis=0), w_all, b_all)

    def split_mods(m):
        layer = [m[:, i * 6 * d:(i + 1) * 6 * d] for i in range(DEPTH)]
        kvm = m[:, DEPTH * 6 * d:]
        return layer, (kvm[:, :d], kvm[:, d:])

    mods_p, kvm_p = split_mods(m_all[:bp])
    mods_s, kvm_s = split_mods(m_all[bp:])

    def attend_prompt(q, k, v, fcum):
        return fox_prompt_attention(q, k, v, fcum, bp, t_p)

    def attend_sample(q, k, v, fcum):
        return fox_sample_attention(q, k, v, fcum, cache_k, cache_v, cache_logf, page_table, bs, t_s)

    shift0_p = jnp.zeros((N_A, bp, d), F32)
    wkv0_p = jnp.zeros((N_A, bp, N_HEADS, HEAD_DIM, HEAD_DIM), F32)
    moe_w32 = (moe_w_gate, moe_w_up, moe_w_down)
    y_p, shift_p, wkv_p, k_p, v_p, lf_p = _trunk(x_prompt, mods_p, kvm_p, shift0_p, wkv0_p, attend_prompt,
                                                 p, moe_w, moe_w32)
    y_s, shift_s, wkv_s, k_s, v_s, lf_s = _trunk(x_sample, mods_s, kvm_s, state_shift, state_wkv,
                                                 attend_sample, p, moe_w, moe_w32)
    return (y_p, y_s, shift_p, wkv_p, k_p, v_p, lf_p, shift_s, wkv_s, k_s, v_s, lf_s)
```
